```python
import jax
import jax.numpy as jnp
from jax import lax
import numpy as np

D_MODEL = 2048
BATCH = 4
SEQ = 2048
DEPTH = 4
DEC_BATCH = 8
DEC_SEQ = 4
PAST_LEN = 16384
PAGE_SIZE = 128

N_AB_LAYERS = (DEPTH + 1) // 2
N_ATTN_LAYERS = DEPTH // 2
D_A = D_MODEL // 2
D_B = D_MODEL // 2
CONV_A_WIDTH = 3
CONV_B_WIDTH = 31
N_HEADS = 16
HEAD_DIM = D_MODEL // N_HEADS
Q_BLOCK = 128
PEER_HEADS = 8
PEER_NKEYS = 128
PEER_EXPERTS = PEER_NKEYS * PEER_NKEYS
PEER_DKEY = 256
PEER_TOPK = 16
PEER_TOKEN_BLOCK = 128
NORM_EPS = 1e-6

kernel_name = 'hybrid_shortconv_conformer_fox_peer_step'


def rms_norm(x, g):
    xf = x.astype(jnp.float32)
    y = xf * lax.rsqrt(jnp.mean(xf * xf, axis=-1, keepdims=True) + NORM_EPS)
    return (y * g.astype(jnp.float32)).astype(x.dtype)


def layer_norm(x, g, b):
    xf = x.astype(jnp.float32)
    mu = jnp.mean(xf, axis=-1, keepdims=True)
    var = jnp.mean(jnp.square(xf - mu), axis=-1, keepdims=True)
    y = (xf - mu) * lax.rsqrt(var + NORM_EPS)
    return (y * g.astype(jnp.float32) + b.astype(jnp.float32)).astype(x.dtype)


def adaln_modulation(c, w_ada, b_ada):
    mod = jax.nn.silu(c) @ w_ada + b_ada
    return jnp.split(mod[:, None, :], 6, axis=-1)


def modulate(x, g, shift, scale):
    return rms_norm(x, g) * (1 + scale) + shift


def causal_dwconv(u, hist, w):
    width = w.shape[0]
    full = jnp.concatenate([hist.astype(u.dtype), u], axis=1)
    out = lax.conv_general_dilated(full, w[:, None, :].astype(u.dtype), window_strides=(1,), padding='VALID',
                                   dimension_numbers=('NWC', 'WIO', 'NWC'), feature_group_count=u.shape[-1])
    return out, full[:, full.shape[1] - (width - 1):]


def conv_mixer(h, hist_a, hist_b, w_in, conv_a_w, conv_b_w, conv_b_bias, ln_g, ln_b, w_out):
    proj = h @ w_in
    b_gate, c_gate, x_a, v_b, g_b = jnp.split(proj, [D_A, 2 * D_A, 3 * D_A, 3 * D_A + D_B], axis=-1)
    conv_a, new_a = causal_dwconv(c_gate * x_a, hist_a, conv_a_w)
    y_a = b_gate * conv_a
    glu = v_b * jax.nn.sigmoid(g_b)
    conv_b, new_b = causal_dwconv(glu, hist_b, conv_b_w)
    y_b = jax.nn.silu(layer_norm(conv_b + conv_b_bias, ln_g, ln_b))
    y = jnp.concatenate([y_a, y_b], axis=-1) @ w_out
    return y, new_a, new_b


def fox_project(h, w_qkvf, b_f, q_norm_g, k_norm_g):
    n, L, _ = h.shape
    proj = h @ w_qkvf
    q, k, v, f = jnp.split(proj, [D_MODEL, 2 * D_MODEL, 3 * D_MODEL], axis=-1)
    q = rms_norm(q.reshape(n, L, N_HEADS, HEAD_DIM), q_norm_g)
    k = rms_norm(k.reshape(n, L, N_HEADS, HEAD_DIM), k_norm_g)
    v = v.reshape(n, L, N_HEADS, HEAD_DIM)
    log_f = jax.nn.log_sigmoid(f.astype(jnp.float32) + b_f.astype(jnp.float32))
    return q, k, v, log_f


def fox_attention_prompt(q, k, v, log_f):
    n, L, H, Dh = q.shape
    n_blk = L // Q_BLOCK
    scale = HEAD_DIM ** -0.5
    F = jnp.transpose(lax.cumsum(log_f, axis=1), (0, 2, 1))
    key_pos = jnp.arange(L)

    def block(i):
        start = i * Q_BLOCK
        qb = lax.dynamic_slice_in_dim(q, start, Q_BLOCK, axis=1)
        Fq = lax.dynamic_slice_in_dim(F, start, Q_BLOCK, axis=2)
        s = jnp.einsum('bqhd,bkhd->bhqk', qb, k, preferred_element_type=jnp.float32) * scale
        s = s + Fq[:, :, :, None] - F[:, :, None, :]
        q_pos = start + jnp.arange(Q_BLOCK)
        s = jnp.where(key_pos[None, :] <= q_pos[:, None], s, -jnp.inf)
        p = jax.nn.softmax(s, axis=-1).astype(v.dtype)
        return jnp.einsum('bhqk,bkhd->bqhd', p, v)

    out = lax.map(block, jnp.arange(n_blk))
    return jnp.transpose(out, (1, 0, 2, 3, 4)).reshape(n, L, H * Dh)


def fox_attention_sample(q, k, v, log_f, k_past, v_past, logf_past):
    n, T, H, Dh = q.shape
    P = k_past.shape[1]
    scale = HEAD_DIM ** -0.5
    lp = logf_past.astype(jnp.float32)
    R = jnp.transpose(lax.cumsum(lp, axis=1, reverse=True) - lp, (0, 2, 1))
    G = jnp.transpose(lax.cumsum(log_f, axis=1), (0, 2, 1))
    s_past = jnp.einsum('bqhd,bkhd->bhqk', q, k_past.astype(q.dtype), preferred_element_type=jnp.float32) * scale
    s_past = s_past + G[:, :, :, None] + R[:, :, None, :]
    s_new = jnp.einsum('bqhd,bkhd->bhqk', q, k, preferred_element_type=jnp.float32) * scale
    s_new = s_new + G[:, :, :, None] - G[:, :, None, :]
    causal = jnp.arange(T)[None, :] <= jnp.arange(T)[:, None]
    s_new = jnp.where(causal, s_new, -jnp.inf)
    p = jax.nn.softmax(jnp.concatenate([s_past, s_new], axis=-1), axis=-1).astype(v.dtype)
    out = (jnp.einsum('bhqk,bkhd->bqhd', p[..., :P], v_past.astype(v.dtype))
           + jnp.einsum('bhqk,bkhd->bqhd', p[..., P:], v))
    return out.reshape(n, T, H * Dh)


def peer_mixer(h, w_q, sub_keys, u_emb, v_emb):
    n, L, D = h.shape
    x = h.reshape(n * L, D)
    T = x.shape[0]
    n_blk = -(-T // PEER_TOKEN_BLOCK)
    pad = n_blk * PEER_TOKEN_BLOCK - T
    xp = jnp.pad(x, ((0, pad), (0, 0))).reshape(n_blk, PEER_TOKEN_BLOCK, D)

    def block(xb):
        t = xb.shape[0]
        q = (xb @ w_q).reshape(t, PEER_HEADS, 2, PEER_DKEY // 2)
        s = jnp.einsum('thpd,hpnd->thpn', q, sub_keys, preferred_element_type=jnp.float32)
        top_s, top_i = lax.top_k(s, PEER_TOPK)
        cand_s = (top_s[:, :, 0, :, None] + top_s[:, :, 1, None, :]).reshape(t, PEER_HEADS, PEER_TOPK * PEER_TOPK)
        cand_i = (top_i[:, :, 0, :, None] * PEER_NKEYS + top_i[:, :, 1, None, :]).reshape(t, PEER_HEADS, PEER_TOPK * PEER_TOPK)
        best_s, best_pos = lax.top_k(cand_s, PEER_TOPK)
        expert_ids = jnp.take_along_axis(cand_i, best_pos, axis=-1)
        gates = jax.nn.softmax(best_s, axis=-1)
        act = jax.nn.gelu(jnp.einsum('thkd,td->thk', u_emb[expert_ids], xb))
        w = (gates * act.astype(jnp.float32)).astype(xb.dtype)
        return jnp.einsum('thk,thkd->td', w, v_emb[expert_ids])

    out = lax.map(block, xp).reshape(n_blk * PEER_TOKEN_BLOCK, D)[:T]
    return out.reshape(n, L, D)


def setup_inputs(seed: int = 0) -> dict:
    key = jax.random.key(seed)
    ks = iter(jax.random.split(key, 48))

    def nrm(shape, s):
        return jax.random.normal(next(ks), shape, jnp.float32) * s

    D = D_MODEL
    n_pages = PAST_LEN // PAGE_SIZE
    n_used = DEC_BATCH * n_pages
    n_pool = n_used + max(1, n_used // 4)
    page_table = jax.random.permutation(next(ks), n_pool)[:n_used].reshape(DEC_BATCH, n_pages).astype(jnp.int32)
    head_bias = jnp.linspace(1.0, 8.0, N_HEADS, dtype=jnp.float32)

    x_prompt = nrm((BATCH, SEQ, D), 1.0)
    x_sample = nrm((DEC_BATCH, DEC_SEQ, D), 1.0)
    state_conv_a = nrm((N_AB_LAYERS, DEC_BATCH, CONV_A_WIDTH - 1, D_A), 1.0)
    state_conv_b = nrm((N_AB_LAYERS, DEC_BATCH, CONV_B_WIDTH - 1, D_B), 1.0)
    cache_k = nrm((N_ATTN_LAYERS, n_pool, PAGE_SIZE, N_HEADS, HEAD_DIM), 1.0)
    cache_v = nrm((N_ATTN_LAYERS, n_pool, PAGE_SIZE, N_HEADS, HEAD_DIM), 1.0)
    cache_logf = jax.nn.log_sigmoid(head_bias + nrm((N_ATTN_LAYERS, n_pool, PAGE_SIZE, N_HEADS), 0.3))
    c_prompt = nrm((BATCH, D), 1.0)
    c_sample = nrm((DEC_BATCH, D), 1.0)

    norm1_g = 1.0 + nrm((DEPTH, D), 0.02)
    norm2_g = 1.0 + nrm((DEPTH, D), 0.02)
    ada_w = nrm((DEPTH, D, 6 * D), 0.5 * D ** -0.5)
    ada_b = nrm((DEPTH, 6 * D), 0.01)

    ab_w_in = nrm((N_AB_LAYERS, D, 3 * D_A + 2 * D_B), D ** -0.5)
    ab_conv_a_w = nrm((N_AB_LAYERS, CONV_A_WIDTH, D_A), CONV_A_WIDTH ** -0.5)
    ab_conv_b_w = nrm((N_AB_LAYERS, CONV_B_WIDTH, D_B), CONV_B_WIDTH ** -0.5)
    ab_conv_b_bias = nrm((N_AB_LAYERS, D_B), 0.01)
    ab_ln_g = 1.0 + nrm((N_AB_LAYERS, D_B), 0.02)
    ab_ln_b = nrm((N_AB_LAYERS, D_B), 0.01)
    ab_w_out = nrm((N_AB_LAYERS, D_A + D_B, D), (D_A + D_B) ** -0.5)

    w_qkv = nrm((N_ATTN_LAYERS, D, 3 * D), D ** -0.5)
    w_f = nrm((N_ATTN_LAYERS, D, N_HEADS), 0.1 * D ** -0.5)
    attn_w_qkvf = jnp.concatenate([w_qkv, w_f], axis=-1)
    attn_b_f = head_bias + nrm((N_ATTN_LAYERS, N_HEADS), 0.1)
    attn_q_norm_g = 1.0 + nrm((N_ATTN_LAYERS, HEAD_DIM), 0.02)
    attn_k_norm_g = 1.0 + nrm((N_ATTN_LAYERS, HEAD_DIM), 0.02)
    attn_w_o = nrm((N_ATTN_LAYERS, D, D), D ** -0.5)

    peer_w_q = nrm((DEPTH, D, PEER_HEADS * PEER_DKEY), D ** -0.5)
    peer_sub_keys = nrm((DEPTH, PEER_HEADS, 2, PEER_NKEYS, PEER_DKEY // 2), (PEER_DKEY // 2) ** -0.5)
    peer_u = nrm((DEPTH, PEER_EXPERTS, D), D ** -0.5)
    peer_v = nrm((DEPTH, PEER_EXPERTS, D), PEER_HEADS ** -0.5)

    return {'x_prompt': x_prompt, 'x_sample': x_sample,
            'state_conv_a': state_conv_a, 'state_conv_b': state_conv_b,
            'cache_k': cache_k, 'cache_v': cache_v, 'cache_logf': cache_logf, 'page_table': page_table,
            'c_prompt': c_prompt, 'c_sample': c_sample,
            'norm1_g': norm1_g, 'norm2_g': norm2_g, 'ada_w': ada_w, 'ada_b': ada_b,
            'ab_w_in': ab_w_in, 'ab_conv_a_w': ab_conv_a_w, 'ab_conv_b_w': ab_conv_b_w,
            'ab_conv_b_bias': ab_conv_b_bias, 'ab_ln_g': ab_ln_g, 'ab_ln_b': ab_ln_b, 'ab_w_out': ab_w_out,
            'attn_w_qkvf': attn_w_qkvf, 'attn_b_f': attn_b_f, 'attn_q_norm_g': attn_q_norm_g,
            'attn_k_norm_g': attn_k_norm_g, 'attn_w_o': attn_w_o,
            'peer_w_q': peer_w_q, 'peer_sub_keys': peer_sub_keys, 'peer_u': peer_u, 'peer_v': peer_v}


def reference(x_prompt, x_sample, state_conv_a, state_conv_b, cache_k, cache_v, cache_logf, page_table,
              c_prompt, c_sample, norm1_g, norm2_g, ada_w, ada_b,
              ab_w_in, ab_conv_a_w, ab_conv_b_w, ab_conv_b_bias, ab_ln_g, ab_ln_b, ab_w_out,
              attn_w_qkvf, attn_b_f, attn_q_norm_g, attn_k_norm_g, attn_w_o,
              peer_w_q, peer_sub_keys, peer_u, peer_v):
    xp, xs = x_prompt, x_sample
    n_p, n_s = xp.shape[0], xs.shape[0]
    conv_a_p, conv_a_s, conv_b_p, conv_b_s = [], [], [], []
    k_p, v_p, lf_p, k_s, v_s, lf_s = [], [], [], [], [], []
    for l in range(DEPTH):
        sh1p, sc1p, g1p, sh2p, sc2p, g2p = adaln_modulation(c_prompt, ada_w[l], ada_b[l])
        sh1s, sc1s, g1s, sh2s, sc2s, g2s = adaln_modulation(c_sample, ada_w[l], ada_b[l])
        hp = modulate(xp, norm1_g[l], sh1p, sc1p)
        hs = modulate(xs, norm1_g[l], sh1s, sc1s)
        i = l // 2
        if l % 2 == 0:
            ab = (ab_w_in[i], ab_conv_a_w[i], ab_conv_b_w[i], ab_conv_b_bias[i], ab_ln_g[i], ab_ln_b[i], ab_w_out[i])
            hist_a0 = jnp.zeros((n_p, CONV_A_WIDTH - 1, D_A), xp.dtype)
            hist_b0 = jnp.zeros((n_p, CONV_B_WIDTH - 1, D_B), xp.dtype)
            yp, na, nb = conv_mixer(hp, hist_a0, hist_b0, *ab)
            conv_a_p.append(na)
            conv_b_p.append(nb)
            ys, na, nb = conv_mixer(hs, state_conv_a[i], state_conv_b[i], *ab)
            conv_a_s.append(na)
            conv_b_s.append(nb)
        else:
            fw = (attn_w_qkvf[i], attn_b_f[i], attn_q_norm_g[i], attn_k_norm_g[i])
            q, k, v, lf = fox_project(hp, *fw)
            yp = fox_attention_prompt(q, k, v, lf) @ attn_w_o[i]
            k_p.append(k)
            v_p.append(v)
            lf_p.append(lf)
            q, k, v, lf = fox_project(hs, *fw)
            k_past = cache_k[i, page_table].reshape(n_s, -1, N_HEADS, HEAD_DIM)
            v_past = cache_v[i, page_table].reshape(n_s, -1, N_HEADS, HEAD_DIM)
            lf_past = cache_logf[i, page_table].reshape(n_s, -1, N_HEADS)
            ys = fox_attention_sample(q, k, v, lf, k_past, v_past, lf_past) @ attn_w_o[i]
            k_s.append(k)
            v_s.append(v)
            lf_s.append(lf)
        xp = xp + g1p * yp
        xs = xs + g1s * ys
        pw = (peer_w_q[l], peer_sub_keys[l], peer_u[l], peer_v[l])
        xp = xp + g2p * peer_mixer(modulate(xp, norm2_g[l], sh2p, sc2p), *pw)
        xs = xs + g2s * peer_mixer(modulate(xs, norm2_g[l], sh2s, sc2s), *pw)
    return (xp, xs, jnp.stack(conv_a_p), jnp.stack(conv_a_s), jnp.stack(conv_b_p), jnp.stack(conv_b_s),
            jnp.stack(k_p), jnp.stack(v_p), jnp.stack(lf_p), jnp.stack(k_s), jnp.stack(v_s), jnp.stack(lf_s))
```

```python
import functools
import math

import jax
import jax.numpy as jnp
from jax import lax
from jax.experimental import pallas as pl
from jax.experimental.pallas import tpu as pltpu

F32 = jnp.float32
BF16 = jnp.bfloat16
NORM_EPS = 1e-6
PEER_TOPK = 16
LANES = 128
SUBLANES = 8
VMEM_LIMIT = 56 * 1024 * 1024
NEG = -1e30
GELU_C = math.sqrt(2.0 / math.pi)


def _cparams(*sem):
    return pltpu.CompilerParams(dimension_semantics=sem, vmem_limit_bytes=VMEM_LIMIT)


def _tile(n, pref):
    if n <= pref:
        return n
    t = pref
    while n % t:
        t //= 2
    return t


def _sigmoid(x):
    return 1.0 / (1.0 + jnp.exp(-x))


def _dot(a, b):
    return jnp.dot(a, b, preferred_element_type=F32)


def _dot_nt(a, b):
    return lax.dot_general(a, b, (((1,), (1,)), ((), ())), preferred_element_type=F32)


def _split3(x):
    hi = x.astype(BF16)
    r1 = x - hi.astype(F32)
    mid = r1.astype(BF16)
    lo = (r1 - mid.astype(F32)).astype(BF16)
    return hi, mid, lo


def _dot_f32_lhs(x, m01):
    hi, mid, lo = _split3(x)
    return _dot(hi, m01) + _dot(mid, m01) + _dot(lo, m01)


def _dot_f32_rhs(m01, x):
    hi, mid, lo = _split3(x)
    return _dot(m01, hi) + _dot(m01, mid) + _dot(m01, lo)


def _iota(shape, dim):
    return lax.broadcasted_iota(jnp.int32, shape, dim)


def _ada_kernel(c_ref, w_ref, b_ref, o_ref):
    c = c_ref[...]
    sc = (c * _sigmoid(c)).astype(BF16)
    o_ref[0] = _dot(sc, w_ref[0].astype(BF16)) + b_ref[0]


def ada_modulation(c_all, ada_w, ada_b):
    depth, d, n = ada_w.shape
    rows = c_all.shape[0]
    tn = _tile(n, 1024)
    return pl.pallas_call(
        _ada_kernel,
        grid=(depth, n // tn),
        in_specs=[pl.BlockSpec((rows, d), lambda l, j: (0, 0)),
                  pl.BlockSpec((1, d, tn), lambda l, j: (l, 0, j)),
                  pl.BlockSpec((1, 1, tn), lambda l, j: (l, 0, j))],
        out_specs=pl.BlockSpec((1, rows, tn), lambda l, j: (l, 0, j)),
        out_shape=jax.ShapeDtypeStruct((depth, rows, n), F32),
        compiler_params=_cparams("arbitrary", "arbitrary"),
        name="ada_modulation",
    )(c_all, ada_w, ada_b.reshape(depth, 1, n))


def _modnorm_kernel(x_ref, g_ref, sh_ref, sc_ref, o_ref):
    x = x_ref[0]
    ms = jnp.mean(x * x, axis=-1, keepdims=True)
    y = x * lax.rsqrt(ms + NORM_EPS) * g_ref[...]
    o_ref[0] = (y * (1.0 + sc_ref[0]) + sh_ref[0]).astype(o_ref.dtype)


def _mod_spec(mod, tl, d):
    if mod.shape[1] == 1:
        return pl.BlockSpec((1, 1, d), lambda b, i, *_: (b, 0, 0))
    return pl.BlockSpec((1, tl, d), lambda b, i, *_: (b, i, 0))


def modnorm(x, g, shift, scale):
    nb, L, d = x.shape
    tl = _tile(L, 512)
    return pl.pallas_call(
        _modnorm_kernel,
        grid=(nb, L // tl),
        in_specs=[pl.BlockSpec((1, tl, d), lambda b, i: (b, i, 0)),
                  pl.BlockSpec((1, d), lambda b, i: (0, 0)),
                  _mod_spec(shift, tl, d), _mod_spec(scale, tl, d)],
        out_specs=pl.BlockSpec((1, tl, d), lambda b, i: (b, i, 0)),
        out_shape=jax.ShapeDtypeStruct((nb, L, d), BF16),
        compiler_params=_cparams("arbitrary", "arbitrary"),
        name="modnorm",
    )(x, g.reshape(1, d), shift, scale)


def _mm_kernel(a_ref, w_ref, o_ref):
    o_ref[0] = _dot(a_ref[0], w_ref[...]).astype(o_ref.dtype)


def _mm_res_kernel(a_ref, w_ref, r_ref, g_ref, o_ref):
    o_ref[0] = r_ref[0] + g_ref[0] * _dot(a_ref[0], w_ref[...])


def matmul(a, w, out_dtype=F32, res=None, gate=None, tl_pref=512, tn_pref=512):
    nb, L, k = a.shape
    n = w.shape[1]
    tl = _tile(L, tl_pref)
    tn = _tile(n, tn_pref)
    in_specs = [pl.BlockSpec((1, tl, k), lambda b, i, j: (b, i, 0)),
                pl.BlockSpec((k, tn), lambda b, i, j: (0, j))]
    args = [a, w]
    body = _mm_kernel
    if res is not None:
        if gate.shape[1] == 1:
            gspec = pl.BlockSpec((1, 1, tn), lambda b, i, j: (b, 0, j))
        else:
            gspec = pl.BlockSpec((1, tl, tn), lambda b, i, j: (b, i, j))
        in_specs += [pl.BlockSpec((1, tl, tn), lambda b, i, j: (b, i, j)), gspec]
        args += [res, gate]
        body = _mm_res_kernel
    return pl.pallas_call(
        body,
        grid=(nb, L // tl, n // tn),
        in_specs=in_specs,
        out_specs=pl.BlockSpec((1, tl, tn), lambda b, i, j: (b, i, j)),
        out_shape=jax.ShapeDtypeStruct((nb, L, n), out_dtype),
        compiler_params=_cparams("arbitrary", "arbitrary", "arbitrary"),
        name="matmul",
    )(*args)


def _gated_add_kernel(x_ref, g_ref, y_ref, o_ref):
    o_ref[0] = x_ref[0] + g_ref[0] * y_ref[0]


def gated_add(x, gate, y):
    nb, L, d = x.shape
    tl = _tile(L, 512)
    spec = pl.BlockSpec((1, tl, d), lambda b, i: (b, i, 0))
    return pl.pallas_call(
        _gated_add_kernel,
        grid=(nb, L // tl),
        in_specs=[spec, _mod_spec(gate, tl, d), spec],
        out_specs=spec,
        out_shape=jax.ShapeDtypeStruct((nb, L, d), F32),
        compiler_params=_cparams("arbitrary", "arbitrary"),
        name="gated_add",
    )(x, gate, y)


CONV_PAD_A = SUBLANES
CONV_PAD_B = 4 * SUBLANES


def _conv_kernel(proj_ref, ha_ref, hb_ref, wa_ref, wb_ref, bias_ref, lng_ref, lnb_ref,
                 y_ref, na_ref, nb_ref, fa, fb, cb, *, tl, da, db, wa_n, wb_n):
    i = pl.program_id(1)
    last = pl.num_programs(1) - 1
    pa, pb = CONV_PAD_A, CONV_PAD_B

    @pl.when(i == 0)
    def _():
        fa[pa - (wa_n - 1):pa, :] = ha_ref[0]
        fb[pb - (wb_n - 1):pb, :] = hb_ref[0]

    b_gate = proj_ref[0, :, 0:da]
    c_gate = proj_ref[0, :, da:2 * da]
    x_a = proj_ref[0, :, 2 * da:3 * da]
    v_b = proj_ref[0, :, 3 * da:3 * da + db]
    g_b = proj_ref[0, :, 3 * da + db:3 * da + 2 * db]
    fa[pa:pa + tl, :] = c_gate * x_a
    fb[pb:pb + tl, :] = v_b * _sigmoid(g_b)

    acc = jnp.zeros((tl, da), F32)
    for k in range(wa_n):
        r0 = pa - (wa_n - 1) + k
        acc = acc + wa_ref[k:k + 1, :] * fa[r0:r0 + tl, :]
    y_ref[0, :, 0:da] = (b_gate * acc).astype(y_ref.dtype)

    for c in range(db // LANES):
        cs = slice(c * LANES, (c + 1) * LANES)
        accb = jnp.zeros((tl, LANES), F32)
        for k in range(wb_n):
            r0 = pb - (wb_n - 1) + k
            accb = accb + wb_ref[k:k + 1, cs] * fb[r0:r0 + tl, cs]
        cb[:, cs] = accb + bias_ref[:, cs]

    z = cb[...]
    mu = jnp.mean(z, axis=-1, keepdims=True)
    zc = z - mu
    var = jnp.mean(zc * zc, axis=-1, keepdims=True)
    zn = zc * lax.rsqrt(var + NORM_EPS) * lng_ref[...] + lnb_ref[...]
    y_ref[0, :, da:da + db] = (zn * _sigmoid(zn)).astype(y_ref.dtype)

    @pl.when(i == last)
    def _():
        na_ref[0] = fa[pa + tl - (wa_n - 1):pa + tl, :]
        nb_ref[0] = fb[pb + tl - (wb_n - 1):pb + tl, :]

    @pl.when(i < last)
    def _():
        fa[pa - (wa_n - 1):pa, :] = fa[pa + tl - (wa_n - 1):pa + tl, :]
        fb[pb - (wb_n - 1):pb, :] = fb[pb + tl - (wb_n - 1):pb + tl, :]


def conv_mixer_mid(proj, hist_a, hist_b, conv_a_w, conv_b_w, conv_b_bias, ln_g, ln_b):
    nb, L, pw = proj.shape
    wa_n, da = conv_a_w.shape
    wb_n, db = conv_b_w.shape
    assert wa_n - 1 <= CONV_PAD_A and wb_n - 1 <= CONV_PAD_B
    tl = _tile(L, 256)
    assert tl == L or tl >= wb_n - 1
    kern = functools.partial(_conv_kernel, tl=tl, da=da, db=db, wa_n=wa_n, wb_n=wb_n)
    full2 = lambda b, i: (0, 0)
    return pl.pallas_call(
        kern,
        grid=(nb, L // tl),
        in_specs=[pl.BlockSpec((1, tl, pw), lambda b, i: (b, i, 0)),
                  pl.BlockSpec((1, wa_n - 1, da), lambda b, i: (b, 0, 0)),
                  pl.BlockSpec((1, wb_n - 1, db), lambda b, i: (b, 0, 0)),
                  pl.BlockSpec((wa_n, da), full2),
                  pl.BlockSpec((wb_n, db), full2),
                  pl.BlockSpec((1, db), full2),
                  pl.BlockSpec((1, db), full2),
                  pl.BlockSpec((1, db), full2)],
        out_specs=[pl.BlockSpec((1, tl, da + db), lambda b, i: (b, i, 0)),
                   pl.BlockSpec((1, wa_n - 1, da), lambda b, i: (b, 0, 0)),
                   pl.BlockSpec((1, wb_n - 1, db), lambda b, i: (b, 0, 0))],
        out_shape=[jax.ShapeDtypeStruct((nb, L, da + db), BF16),
                   jax.ShapeDtypeStruct((nb, wa_n - 1, da), F32),
                   jax.ShapeDtypeStruct((nb, wb_n - 1, db), F32)],
        scratch_shapes=[pltpu.VMEM((CONV_PAD_A + tl, da), F32),
                        pltpu.VMEM((CONV_PAD_B + tl, db), F32),
                        pltpu.VMEM((tl, db), F32)],
        compiler_params=_cparams("arbitrary", "arbitrary"),
        name="conv_mixer_mid",
    )(proj, hist_a, hist_b, conv_a_w, conv_b_w, conv_b_bias.reshape(1, db),
      ln_g.reshape(1, db), ln_b.reshape(1, db))


def _fox_post_kernel(proj_ref, f_ref, bf_ref, qg_ref, kg_ref,
                     qn_ref, kn_ref, kb_ref, v_ref, vb_ref, lf_ref, *, d, nh, dh):
    qg = qg_ref[...]
    kg = kg_ref[...]
    for h in range(nh):
        cs = slice(h * dh, (h + 1) * dh)
        q = proj_ref[0, :, h * dh:(h + 1) * dh]
        qn = q * lax.rsqrt(jnp.mean(q * q, axis=-1, keepdims=True) + NORM_EPS) * qg
        qn_ref[0, :, cs] = qn.astype(qn_ref.dtype)
        k = proj_ref[0, :, d + h * dh:d + (h + 1) * dh]
        kn = k * lax.rsqrt(jnp.mean(k * k, axis=-1, keepdims=True) + NORM_EPS) * kg
        kn_ref[0, :, cs] = kn
        kb_ref[0, :, cs] = kn.astype(kb_ref.dtype)
    v = proj_ref[0, :, 2 * d:3 * d]
    v_ref[0] = v
    vb_ref[0] = v.astype(vb_ref.dtype)
    z = f_ref[0] + bf_ref[...]
    ls = jnp.minimum(z, 0.0) - jnp.log(1.0 + jnp.exp(-jnp.abs(z)))
    lf_ref[0] = ls[:, 0:nh]


def fox_post(proj, f, b_f_pad, q_g, k_g, nh):
    nb, L, d3 = proj.shape
    d = d3 // 3
    dh = d // nh
    tl = _tile(L, 256)
    kern = functools.partial(_fox_post_kernel, d=d, nh=nh, dh=dh)
    row = lambda w: pl.BlockSpec((1, tl, w), lambda b, i: (b, i, 0))
    vec = lambda w: pl.BlockSpec((1, w), lambda b, i: (0, 0))
    return pl.pallas_call(
        kern,
        grid=(nb, L // tl),
        in_specs=[row(d3), row(LANES), vec(LANES), vec(dh), vec(dh)],
        out_specs=[row(d), row(d), row(d), row(d), row(d), row(nh)],
        out_shape=[jax.ShapeDtypeStruct((nb, L, d), BF16),
                   jax.ShapeDtypeStruct((nb, L, d), F32),
                   jax.ShapeDtypeStruct((nb, L, d), BF16),
                   jax.ShapeDtypeStruct((nb, L, d), F32),
                   jax.ShapeDtypeStruct((nb, L, d), BF16),
                   jax.ShapeDtypeStruct((nb, L, nh), F32)],
        compiler_params=_cparams("arbitrary", "arbitrary"),
        name="fox_post",
    )(proj, f, b_f_pad, q_g.reshape(1, dh), k_g.reshape(1, dh))


def _attn_prompt_kernel(q_ref, k_ref, v_ref, lf_ref, o_ref, f_scr, *, L, tq, dh, scale):
    nqb = L // tq
    per = tq // LANES
    tri = (_iota((LANES, LANES), 0) <= _iota((LANES, LANES), 1)).astype(BF16)
    carry = jnp.zeros((1, 1), F32)
    for blk in range(L // LANES):
        x = lf_ref[0, :, blk * LANES:(blk + 1) * LANES]
        xb = jnp.broadcast_to(x, (SUBLANES, LANES))
        cs = _dot_f32_lhs(xb, tri)[0:1, :] + carry
        f_scr[blk // per, :, (blk % per) * LANES:(blk % per + 1) * LANES] = cs
        carry = cs[:, LANES - 1:LANES]

    eye = _iota((tq, tq), 0) == _iota((tq, tq), 1)
    causal = _iota((tq, tq), 1) <= _iota((tq, tq), 0)

    def q_block(qi, _):
        q0 = pl.multiple_of(qi * tq, tq)
        q = q_ref[0, pl.ds(q0, tq), :]
        fq_row = f_scr[qi]
        fq_col = jnp.sum(jnp.where(eye, jnp.broadcast_to(fq_row, (tq, tq)), 0.0),
                         axis=1, keepdims=True)

        def kv_step(j, carry, masked):
            m, l, acc = carry
            k0 = pl.multiple_of(j * tq, tq)
            k = k_ref[0, pl.ds(k0, tq), :]
            v = v_ref[0, pl.ds(k0, tq), :]
            s = _dot_nt(q, k) * scale + (fq_col - f_scr[j])
            if masked:
                s = jnp.where(causal, s, NEG)
            m_new = jnp.maximum(m, jnp.max(s, axis=1, keepdims=True))
            alpha = jnp.exp(m - m_new)
            p = jnp.exp(s - m_new)
            l = alpha * l + jnp.sum(p, axis=1, keepdims=True)
            acc = alpha * acc + _dot(p.astype(BF16), v)
            return m_new, l, acc

        init = (jnp.full((tq, 1), NEG, F32), jnp.zeros((tq, 1), F32), jnp.zeros((tq, dh), F32))
        carry = lax.fori_loop(0, qi, lambda j, c: kv_step(j, c, False), init)
        m, l, acc = kv_step(qi, carry, True)
        o_ref[0, pl.ds(q0, tq), :] = (acc / l).astype(o_ref.dtype)
        return 0

    lax.fori_loop(0, nqb, q_block, 0)


def attn_prompt(qn, kb, vb, lf_t, nh):
    nb, L, d = qn.shape
    dh = d // nh
    tq = _tile(L, 256)
    assert tq % LANES == 0
    kern = functools.partial(_attn_prompt_kernel, L=L, tq=tq, dh=dh, scale=dh ** -0.5)
    hd = pl.BlockSpec((1, L, dh), lambda b, h: (b, 0, h))
    return pl.pallas_call(
        kern,
        grid=(nb, nh),
        in_specs=[hd, hd, hd, pl.BlockSpec((1, 1, L), lambda b, h: (b * nh + h, 0, 0))],
        out_specs=hd,
        out_shape=jax.ShapeDtypeStruct((nb, L, d), BF16),
        scratch_shapes=[pltpu.VMEM((L // tq, 1, tq), F32)],
        compiler_params=_cparams("arbitrary", "arbitrary"),
        name="attn_prompt",
    )(qn, kb, vb, lf_t)


def _attn_sample_kernel(pt_ref, q_ref, kc_ref, vc_ref, lfc_ref, kn_ref, vn_ref, lfn_ref,
                        o_ref, qbig, m_scr, l_scr, acc, car, *, t_new, nh, dh, page, scale):
    del pt_ref
    p = pl.program_id(1)
    last = pl.num_programs(1) - 1
    rows = LANES
    hd = nh * dh

    @pl.when(p == 0)
    def _():
        qbig[...] = jnp.zeros(qbig.shape, qbig.dtype)
        keep = _iota((nh, hd), 0) == _iota((nh, hd), 1) // dh
        for t in range(t_new):
            qrow = jnp.broadcast_to(q_ref[0, t:t + 1, :].astype(F32), (nh, hd))
            qbig[t * nh:(t + 1) * nh, :] = jnp.where(keep, qrow, 0.0).astype(qbig.dtype)
        m_scr[...] = jnp.full(m_scr.shape, NEG, F32)
        l_scr[...] = jnp.zeros(l_scr.shape, F32)
        acc[...] = jnp.zeros(acc.shape, F32)
        car[...] = jnp.zeros(car.shape, F32)

    def online(s, v):
        m = m_scr[...]
        m_new = jnp.maximum(m, jnp.max(s, axis=1, keepdims=True))
        alpha = jnp.exp(m - m_new)
        pr = jnp.exp(s - m_new)
        l_scr[...] = alpha * l_scr[...] + jnp.sum(pr, axis=1, keepdims=True)
        acc[...] = alpha * acc[...] + _dot(pr.astype(BF16), v)
        m_scr[...] = m_new

    lp_t = lfc_ref[0, 0]
    expand = (_iota((rows, nh), 0) % nh == _iota((rows, nh), 1)).astype(BF16)
    lp_rows = _dot_f32_rhs(expand, lp_t)
    later = (_iota((page, page), 0) > _iota((page, page), 1)).astype(BF16)
    r_bias = _dot_f32_lhs(lp_rows, later) + car[...]
    car[...] = car[...] + jnp.sum(lp_rows, axis=1, keepdims=True)
    kp = kc_ref[0, 0].astype(BF16)
    vp = vc_ref[0, 0].astype(BF16)
    online(_dot_nt(qbig[...], kp) * scale + r_bias, vp)

    @pl.when(p == last)
    def _():
        tri = (_iota((LANES, LANES), 0) <= _iota((LANES, LANES), 1)).astype(BF16)
        g_mat = _dot_f32_lhs(lfn_ref[0], tri)
        s = _dot_nt(qbig[...], kn_ref[0]) * scale - g_mat
        tcol = _iota((rows, LANES), 1)
        trow = _iota((rows, LANES), 0) // nh
        s = jnp.where((tcol <= trow) & (tcol < t_new), s, NEG)
        online(s, vn_ref[0])
        out = acc[...] / l_scr[...]
        keep = _iota((rows, hd), 0) % nh == _iota((rows, hd), 1) // dh
        out = jnp.where(keep, out, 0.0)
        for t in range(t_new):
            o_ref[0, t:t + 1, :] = jnp.sum(out[t * nh:(t + 1) * nh, :], axis=0,
                                           keepdims=True).astype(o_ref.dtype)


def attn_sample(layer, qn, kb, vb, lf_new, cache_k, cache_v, cache_lf_t, page_table, nh):
    nb, t_new, hd = qn.shape
    dh = hd // nh
    n_pages = page_table.shape[1]
    page = cache_k.shape[2]
    assert t_new * nh <= LANES and page == LANES
    pad = lambda a: jnp.pad(a, ((0, 0), (0, LANES - t_new), (0, 0)))
    lfn = jnp.transpose(lf_new, (0, 2, 1))
    lfn = jnp.broadcast_to(lfn[:, None], (nb, t_new, nh, t_new)).reshape(nb, t_new * nh, t_new)
    lfn = jnp.pad(lfn, ((0, 0), (0, LANES - t_new * nh), (0, LANES - t_new)))
    kern = functools.partial(_attn_sample_kernel, t_new=t_new, nh=nh, dh=dh, page=page,
                             scale=dh ** -0.5)
    rev = lambda b, p, pt: pt[b, n_pages - 1 - p]
    per_b = lambda r, w: pl.BlockSpec((1, r, w), lambda b, p, pt: (b, 0, 0))
    grid_spec = pltpu.PrefetchScalarGridSpec(
        num_scalar_prefetch=1,
        grid=(nb, n_pages),
        in_specs=[per_b(t_new, hd),
                  pl.BlockSpec((1, 1, page, hd), lambda b, p, pt: (layer, rev(b, p, pt), 0, 0)),
                  pl.BlockSpec((1, 1, page, hd), lambda b, p, pt: (layer, rev(b, p, pt), 0, 0)),
                  pl.BlockSpec((1, 1, nh, page), lambda b, p, pt: (layer, rev(b, p, pt), 0, 0)),
                  per_b(LANES, hd), per_b(LANES, hd), per_b(LANES, LANES)],
        out_specs=per_b(t_new, hd),
        scratch_shapes=[pltpu.VMEM((LANES, hd), BF16),
                        pltpu.VMEM((LANES, 1), F32),
                        pltpu.VMEM((LANES, 1), F32),
                        pltpu.VMEM((LANES, hd), F32),
                        pltpu.VMEM((LANES, 1), F32)])
    return pl.pallas_call(
        kern,
        grid_spec=grid_spec,
        out_shape=jax.ShapeDtypeStruct((nb, t_new, hd), BF16),
        compiler_params=_cparams("arbitrary", "arbitrary"),
        name="attn_sample",
    )(page_table, qn, cache_k, cache_v, cache_lf_t, pad(kb), pad(vb), lfn)


def _staircase(k):
    return [(r0, r1) for r0 in range(k) for r1 in range(k) if (r0 + 1) * (r1 + 1) <= k]


def _peer_select_kernel(qt_ref, keys_ref, s0_ref, s1_ref, aux_ref, cand_scr, *, dk, topk):
    ts = qt_ref.shape[1]
    pairs = _staircase(topk)
    cand_scr[...] = jnp.full(cand_scr.shape, -jnp.inf, F32)

    def head(h, _):
        tops = []
        for half, s_ref in ((0, s0_ref), (1, s1_ref)):
            r0 = pl.multiple_of((h * 2 + half) * dk, dk)
            s = _dot(keys_ref[h, half], qt_ref[pl.ds(r0, dk), :])
            s_ref[h] = s
            vals = []
            x = s
            for _r in range(topk):
                mx = jnp.max(x, axis=0, keepdims=True)
                vals.append(mx)
                x = jnp.where(x == mx, -jnp.inf, x)
            tops.append(vals)
        a, b = tops
        for idx, (r0, r1) in enumerate(pairs):
            cand_scr[idx:idx + 1, :] = a[r0] + b[r1]
        cand = cand_scr[...]
        x = cand
        for _r in range(topk - 1):
            mx = jnp.max(x, axis=0, keepdims=True)
            x = jnp.where(x == mx, -jnp.inf, x)
        theta = jnp.max(x, axis=0, keepdims=True)
        m0, m1 = a[0], b[0]
        z = jnp.sum(jnp.where(cand >= theta, jnp.exp(cand - (m0 + m1)), 0.0),
                    axis=0, keepdims=True)
        aux_ref[h, 0:1, :] = theta
        aux_ref[h, 1:2, :] = m0
        aux_ref[h, 2:3, :] = m1
        aux_ref[h, 3:4, :] = 1.0 / z
        return 0

    lax.fori_loop(0, keys_ref.shape[0], head, 0)


def peer_select(qt, keys):
    nh, _, nk, dk = keys.shape
    tp = qt.shape[1]
    ts = _tile(tp, 256)
    kern = functools.partial(_peer_select_kernel, dk=dk, topk=PEER_TOPK)
    n_cand = -(-len(_staircase(PEER_TOPK)) // SUBLANES) * SUBLANES
    s_spec = pl.BlockSpec((nh, nk, ts), lambda i: (0, 0, i))
    return pl.pallas_call(
        kern,
        grid=(tp // ts,),
        in_specs=[pl.BlockSpec((nh * 2 * dk, ts), lambda i: (0, i)),
                  pl.BlockSpec((nh, 2, nk, dk), lambda i: (0, 0, 0, 0))],
        out_specs=[s_spec, s_spec, pl.BlockSpec((nh, 4, ts), lambda i: (0, 0, i))],
        out_shape=[jax.ShapeDtypeStruct((nh, nk, tp), F32),
                   jax.ShapeDtypeStruct((nh, nk, tp), F32),
                   jax.ShapeDtypeStruct((nh, 4, tp), F32)],
        scratch_shapes=[pltpu.VMEM((n_cand, ts), F32)],
        compiler_params=_cparams("arbitrary"),
        name="peer_select",
    )(qt, keys)


def _peer_dense_kernel(xt_ref, u_ref, vt_ref, s0_ref, s1_ref, aux_ref, o_ref,
                       bz, wact, acc, *, nh, nk, nib):
    e = pl.program_id(1)
    last = pl.num_programs(1) - 1

    @pl.when(e == 0)
    def _():
        acc[...] = jnp.zeros(acc.shape, F32)
        for h in range(nh):
            bz[h] = jnp.exp(s1_ref[h] - aux_ref[h, 2:3, :]) * aux_ref[h, 3:4, :]

    act = _dot(u_ref[...], xt_ref[...])
    g = 0.5 * act * (1.0 + jnp.tanh(GELU_C * (act + 0.044715 * (act * act * act))))
    for ib in range(nib):
        w = jnp.zeros((nk, act.shape[1]), F32)
        for h in range(nh):
            s0row = s0_ref[h, ib:ib + 1, :]
            a = jnp.exp(s0row - aux_ref[h, 1:2, :])
            ssum = s0row + s1_ref[h]
            w = w + jnp.where(ssum >= aux_ref[h, 0:1, :], a * bz[h], 0.0)
        wact[ib * nk:(ib + 1) * nk, :] = (w * g[ib * nk:(ib + 1) * nk, :]).astype(wact.dtype)
    acc[...] += _dot(vt_ref[...], wact[...])

    @pl.when(e == last)
    def _():
        o_ref[...] = acc[...]


def peer_dense(xt, u_b, vt_b, s0t, s1t, aux, tm_pref=512, ec_pref=1024):
    d, tp = xt.shape
    ne = u_b.shape[0]
    nh, nk, _ = s0t.shape
    tm = _tile(tp, tm_pref)
    ec = _tile(ne, ec_pref)
    nib = ec // nk
    assert nib % SUBLANES == 0 or nib == nk
    kern = functools.partial(_peer_dense_kernel, nh=nh, nk=nk, nib=nib)
    return pl.pallas_call(
        kern,
        grid=(tp // tm, ne // ec),
        in_specs=[pl.BlockSpec((d, tm), lambda i, e: (0, i)),
                  pl.BlockSpec((ec, d), lambda i, e: (e, 0)),
                  pl.BlockSpec((d, ec), lambda i, e: (0, e)),
                  pl.BlockSpec((nh, nib, tm), lambda i, e: (0, e, i)),
                  pl.BlockSpec((nh, nk, tm), lambda i, e: (0, 0, i)),
                  pl.BlockSpec((nh, 4, tm), lambda i, e: (0, 0, i))],
        out_specs=pl.BlockSpec((d, tm), lambda i, e: (0, i)),
        out_shape=jax.ShapeDtypeStruct((d, tp), F32),
        scratch_shapes=[pltpu.VMEM((nh, nk, tm), F32),
                        pltpu.VMEM((ec, tm), BF16),
                        pltpu.VMEM((d, tm), F32)],
        compiler_params=_cparams("arbitrary", "arbitrary"),
        name="peer_dense",
    )(xt, u_b, vt_b, s0t, s1t, aux)


def peer_mixer_t(ht, wq_t, keys_b, u_b, vt_b):
    qt = matmul(wq_t[None], ht, out_dtype=BF16)[0]
    s0t, s1t, aux = peer_select(qt, keys_b)
    return peer_dense(ht, u_b, vt_b, s0t, s1t, aux)


def kernel(x_prompt, x_sample, state_conv_a, state_conv_b, cache_k, cache_v, cache_logf, page_table, c_prompt, c_sample, norm1_g, norm2_g, ada_w, ada_b, ab_w_in, ab_conv_a_w, ab_conv_b_w, ab_conv_b_bias, ab_ln_g, ab_ln_b, ab_w_out, attn_w_qkvf, attn_b_f, attn_q_norm_g, attn_k_norm_g, attn_w_o, peer_w_q, peer_sub_keys, peer_u, peer_v):
    n_p, seq, d = x_prompt.shape
    n_s, t_new, _ = x_sample.shape
    depth = ada_w.shape[0]
    nh = attn_b_f.shape[1]
    da = ab_conv_a_w.shape[2]
    db = ab_conv_b_w.shape[2]
    wa_n = ab_conv_a_w.shape[1]
    wb_n = ab_conv_b_w.shape[1]
    n_layers_attn, n_pool, page = cache_k.shape[:3]
    ts_tok = n_s * t_new
    tp_s = -(-ts_tok // LANES) * LANES

    n_c = n_p + n_s
    c_rows = -(-n_c // SUBLANES) * SUBLANES
    c_all = jnp.pad(jnp.concatenate([c_prompt, c_sample], axis=0), ((0, c_rows - n_c), (0, 0)))
    mod_all = ada_modulation(c_all, ada_w, ada_b)

    xp = x_prompt
    xs = x_sample.reshape(1, ts_tok, d)
    cache_k4 = cache_k.reshape(n_layers_attn, n_pool, page, d)
    cache_v4 = cache_v.reshape(n_layers_attn, n_pool, page, d)
    cache_lf_t = jnp.transpose(cache_logf, (0, 1, 3, 2))

    conv_a_p, conv_a_s, conv_b_p, conv_b_s = [], [], [], []
    k_p, v_p, lf_p, k_s, v_s, lf_s = [], [], [], [], [], []
    for l in range(depth):
        mods_p = [mod_all[l, :n_p, k * d:(k + 1) * d][:, None, :] for k in range(6)]
        mods_s = [jnp.repeat(mod_all[l, n_p:n_c, k * d:(k + 1) * d], t_new, axis=0)[None]
                  for k in range(6)]
        sh1p, sc1p, g1p, sh2p, sc2p, g2p = mods_p
        sh1s, sc1s, g1s, sh2s, sc2s, g2s = mods_s
        hp = modnorm(xp, norm1_g[l], sh1p, sc1p)
        hs = modnorm(xs, norm1_g[l], sh1s, sc1s)
        i = l // 2
        if l % 2 == 0:
            w_in = ab_w_in[i].astype(BF16)
            w_out = ab_w_out[i].astype(BF16)
            cw = (ab_conv_a_w[i], ab_conv_b_w[i], ab_conv_b_bias[i], ab_ln_g[i], ab_ln_b[i])
            proj_p = matmul(hp, w_in)
            yp, na, nb = conv_mixer_mid(proj_p, jnp.zeros((n_p, wa_n - 1, da), F32),
                                        jnp.zeros((n_p, wb_n - 1, db), F32), *cw)
            conv_a_p.append(na)
            conv_b_p.append(nb)
            xp = matmul(yp, w_out, res=xp, gate=g1p)
            proj_s = matmul(hs, w_in).reshape(n_s, t_new, -1)
            ys, na, nb = conv_mixer_mid(proj_s, state_conv_a[i], state_conv_b[i], *cw)
            conv_a_s.append(na)
            conv_b_s.append(nb)
            xs = matmul(ys.reshape(1, ts_tok, da + db), w_out, res=xs, gate=g1s)
        else:
            w_qkv = attn_w_qkvf[i][:, :3 * d].astype(BF16)
            w_f = jnp.pad(attn_w_qkvf[i][:, 3 * d:], ((0, 0), (0, LANES - nh))).astype(BF16)
            b_f = jnp.pad(attn_b_f[i], (0, LANES - nh)).reshape(1, LANES)
            w_o = attn_w_o[i].astype(BF16)
            qn, kn, kb, v, vb, lf = fox_post(matmul(hp, w_qkv), matmul(hp, w_f), b_f,
                                             attn_q_norm_g[i], attn_k_norm_g[i], nh)
            lf_t = jnp.transpose(lf, (0, 2, 1)).reshape(n_p * nh, 1, seq)
            op = attn_prompt(qn, kb, vb, lf_t, nh)
            xp = matmul(op, w_o, res=xp, gate=g1p)
            k_p.append(kn.reshape(n_p, seq, nh, d // nh))
            v_p.append(v.reshape(n_p, seq, nh, d // nh))
            lf_p.append(lf)
            qn, kn, kb, v, vb, lf = fox_post(matmul(hs, w_qkv), matmul(hs, w_f), b_f,
                                             attn_q_norm_g[i], attn_k_norm_g[i], nh)
            r3 = lambda a: a.reshape(n_s, t_new, a.shape[-1])
            os_ = attn_sample(i, r3(qn), r3(kb), r3(vb), r3(lf), cache_k4, cache_v4, cache_lf_t,
                              page_table, nh)
            xs = matmul(os_.reshape(1, ts_tok, d), w_o, res=xs, gate=g1s)
            k_s.append(kn.reshape(n_s, t_new, nh, d // nh))
            v_s.append(v.reshape(n_s, t_new, nh, d // nh))
            lf_s.append(r3(lf))
        wq_t = jnp.transpose(peer_w_q[l]).astype(BF16)
        keys_b = peer_sub_keys[l].astype(BF16)
        u_b = peer_u[l].astype(BF16)
        vt_b = jnp.transpose(peer_v[l]).astype(BF16)
        h2p = modnorm(xp, norm2_g[l], sh2p, sc2p)
        yt = peer_mixer_t(jnp.transpose(h2p.reshape(n_p * seq, d)), wq_t, keys_b, u_b, vt_b)
        xp = gated_add(xp, g2p, jnp.transpose(yt).reshape(n_p, seq, d))
        h2s = modnorm(xs, norm2_g[l], sh2s, sc2s)
        h2s_t = jnp.pad(jnp.transpose(h2s[0]), ((0, 0), (0, tp_s - ts_tok)))
        yt = peer_mixer_t(h2s_t, wq_t, keys_b, u_b, vt_b)
        xs = gated_add(xs, g2s, jnp.transpose(yt[:, :ts_tok])[None])

    return (xp, xs.reshape(n_s, t_new, d), jnp.stack(conv_a_p), jnp.stack(conv_a_s),
            jnp.stack(conv_b_p), jnp.stack(conv_b_s),
            jnp.stack(k_p), jnp.stack(v_p), jnp.stack(lf_p),
            jnp.stack(k_s), jnp.stack(v_s), jnp.stack(lf_s))
```

```python
import functools
import math

import jax
import jax.numpy as jnp
from jax import lax
from jax.experimental import pallas as pl
from jax.experimental.pallas import tpu as pltpu

F32 = jnp.float32
BF16 = jnp.bfloat16
NORM_EPS = 1e-6
PEER_TOPK = 16
LANES = 128
SUBLANES = 8
VMEM_LIMIT = 56 * 1024 * 1024
NEG = -1e30
GELU_C = math.sqrt(2.0 / math.pi)


def _cparams(*sem):
    return pltpu.CompilerParams(dimension_semantics=sem, vmem_limit_bytes=VMEM_LIMIT)


def _tile(n, pref):
    if n <= pref:
        return n
    t = pref
    while n % t:
        t //= 2
    return t


def _sigmoid(x):
    return 1.0 / (1.0 + jnp.exp(-x))


def _dot(a, b):
    return jnp.dot(a, b, preferred_element_type=F32)


def _dot_nt(a, b):
    return lax.dot_general(a, b, (((1,), (1,)), ((), ())), preferred_element_type=F32)


def _split3(x):
    hi = x.astype(BF16)
    r1 = x - hi.astype(F32)
    mid = r1.astype(BF16)
    lo = (r1 - mid.astype(F32)).astype(BF16)
    return hi, mid, lo


def _dot_f32_lhs(x, m01):
    hi, mid, lo = _split3(x)
    return _dot(hi, m01) + _dot(mid, m01) + _dot(lo, m01)


def _dot_f32_rhs(m01, x):
    hi, mid, lo = _split3(x)
    return _dot(m01, hi) + _dot(m01, mid) + _dot(m01, lo)


def _iota(shape, dim):
    return lax.broadcasted_iota(jnp.int32, shape, dim)


def _ada_kernel(c_ref, w_ref, b_ref, o_ref):
    c = c_ref[...]
    sc = (c * _sigmoid(c)).astype(BF16)
    o_ref[0] = _dot(sc, w_ref[0].astype(BF16)) + b_ref[0]


def ada_modulation(c_all, ada_w, ada_b):
    depth, d, n = ada_w.shape
    rows = c_all.shape[0]
    tn = _tile(n, 1024)
    return pl.pallas_call(
        _ada_kernel,
        grid=(depth, n // tn),
        in_specs=[pl.BlockSpec((rows, d), lambda l, j: (0, 0)),
                  pl.BlockSpec((1, d, tn), lambda l, j: (l, 0, j)),
                  pl.BlockSpec((1, 1, tn), lambda l, j: (l, 0, j))],
        out_specs=pl.BlockSpec((1, rows, tn), lambda l, j: (l, 0, j)),
        out_shape=jax.ShapeDtypeStruct((depth, rows, n), F32),
        compiler_params=_cparams("arbitrary", "arbitrary"),
        name="ada_modulation",
    )(c_all, ada_w, ada_b.reshape(depth, 1, n))


def _modnorm_kernel(x_ref, g_ref, sh_ref, sc_ref, o_ref):
    x = x_ref[0]
    ms = jnp.mean(x * x, axis=-1, keepdims=True)
    y = x * lax.rsqrt(ms + NORM_EPS) * g_ref[...]
    o_ref[0] = (y * (1.0 + sc_ref[0]) + sh_ref[0]).astype(o_ref.dtype)


def _mod_spec(mod, tl, d):
    if mod.shape[1] == 1:
        return pl.BlockSpec((1, 1, d), lambda b, i, *_: (b, 0, 0))
    return pl.BlockSpec((1, tl, d), lambda b, i, *_: (b, i, 0))


def modnorm(x, g, shift, scale):
    nb, L, d = x.shape
    tl = _tile(L, 512)
    return pl.pallas_call(
        _modnorm_kernel,
        grid=(nb, L // tl),
        in_specs=[pl.BlockSpec((1, tl, d), lambda b, i: (b, i, 0)),
                  pl.BlockSpec((1, d), lambda b, i: (0, 0)),
                  _mod_spec(shift, tl, d), _mod_spec(scale, tl, d)],
        out_specs=pl.BlockSpec((1, tl, d), lambda b, i: (b, i, 0)),
        out_shape=jax.ShapeDtypeStruct((nb, L, d), BF16),
        compiler_params=_cparams("arbitrary", "arbitrary"),
        name="modnorm",
    )(x, g.reshape(1, d), shift, scale)


def _mm_kernel(a_ref, w_ref, o_ref):
    o_ref[0] = _dot(a_ref[0], w_ref[...]).astype(o_ref.dtype)


def _mm_res_kernel(a_ref, w_ref, r_ref, g_ref, o_ref):
    o_ref[0] = r_ref[0] + g_ref[0] * _dot(a_ref[0], w_ref[...])


def matmul(a, w, out_dtype=F32, res=None, gate=None, tl_pref=1024, tn_pref=512):
    nb, L, k = a.shape
    n = w.shape[1]
    tl = _tile(L, tl_pref)
    tn = _tile(n, tn_pref)
    in_specs = [pl.BlockSpec((1, tl, k), lambda b, i, j: (b, i, 0)),
                pl.BlockSpec((k, tn), lambda b, i, j: (0, j))]
    args = [a, w]
    body = _mm_kernel
    if res is not None:
        if gate.shape[1] == 1:
            gspec = pl.BlockSpec((1, 1, tn), lambda b, i, j: (b, 0, j))
        else:
            gspec = pl.BlockSpec((1, tl, tn), lambda b, i, j: (b, i, j))
        in_specs += [pl.BlockSpec((1, tl, tn), lambda b, i, j: (b, i, j)), gspec]
        args += [res, gate]
        body = _mm_res_kernel
    return pl.pallas_call(
        body,
        grid=(nb, L // tl, n // tn),
        in_specs=in_specs,
        out_specs=pl.BlockSpec((1, tl, tn), lambda b, i, j: (b, i, j)),
        out_shape=jax.ShapeDtypeStruct((nb, L, n), out_dtype),
        compiler_params=_cparams("arbitrary", "arbitrary", "arbitrary"),
        name="matmul",
    )(*args)


def _gated_add_kernel(x_ref, g_ref, y_ref, o_ref):
    o_ref[0] = x_ref[0] + g_ref[0] * y_ref[0]


def gated_add(x, gate, y):
    nb, L, d = x.shape
    tl = _tile(L, 512)
    spec = pl.BlockSpec((1, tl, d), lambda b, i: (b, i, 0))
    return pl.pallas_call(
        _gated_add_kernel,
        grid=(nb, L // tl),
        in_specs=[spec, _mod_spec(gate, tl, d), spec],
        out_specs=spec,
        out_shape=jax.ShapeDtypeStruct((nb, L, d), F32),
        compiler_params=_cparams("arbitrary", "arbitrary"),
        name="gated_add",
    )(x, gate, y)


CONV_PAD_A = SUBLANES
CONV_PAD_B = 4 * SUBLANES


def _conv_kernel(proj_ref, ha_ref, hb_ref, wa_ref, wb_ref, bias_ref, lng_ref, lnb_ref,
                 y_ref, na_ref, nb_ref, fa, fb, cb, *, tl, da, db, wa_n, wb_n):
    i = pl.program_id(1)
    last = pl.num_programs(1) - 1
    pa, pb = CONV_PAD_A, CONV_PAD_B

    @pl.when(i == 0)
    def _():
        fa[pa - (wa_n - 1):pa, :] = ha_ref[0]
        fb[pb - (wb_n - 1):pb, :] = hb_ref[0]

    b_gate = proj_ref[0, :, 0:da]
    c_gate = proj_ref[0, :, da:2 * da]
    x_a = proj_ref[0, :, 2 * da:3 * da]
    v_b = proj_ref[0, :, 3 * da:3 * da + db]
    g_b = proj_ref[0, :, 3 * da + db:3 * da + 2 * db]
    fa[pa:pa + tl, :] = c_gate * x_a
    fb[pb:pb + tl, :] = v_b * _sigmoid(g_b)

    acc = jnp.zeros((tl, da), F32)
    for k in range(wa_n):
        r0 = pa - (wa_n - 1) + k
        acc = acc + wa_ref[k:k + 1, :] * fa[r0:r0 + tl, :]
    y_ref[0, :, 0:da] = (b_gate * acc).astype(y_ref.dtype)

    for c in range(db // LANES):
        cs = slice(c * LANES, (c + 1) * LANES)
        accb = jnp.zeros((tl, LANES), F32)
        for k in range(wb_n):
            r0 = pb - (wb_n - 1) + k
            accb = accb + wb_ref[k:k + 1, cs] * fb[r0:r0 + tl, cs]
        cb[:, cs] = accb + bias_ref[:, cs]

    z = cb[...]
    mu = jnp.mean(z, axis=-1, keepdims=True)
    zc = z - mu
    var = jnp.mean(zc * zc, axis=-1, keepdims=True)
    zn = zc * lax.rsqrt(var + NORM_EPS) * lng_ref[...] + lnb_ref[...]
    y_ref[0, :, da:da + db] = (zn * _sigmoid(zn)).astype(y_ref.dtype)

    @pl.when(i == last)
    def _():
        na_ref[0] = fa[pa + tl - (wa_n - 1):pa + tl, :]
        nb_ref[0] = fb[pb + tl - (wb_n - 1):pb + tl, :]

    @pl.when(i < last)
    def _():
        fa[pa - (wa_n - 1):pa, :] = fa[pa + tl - (wa_n - 1):pa + tl, :]
        fb[pb - (wb_n - 1):pb, :] = fb[pb + tl - (wb_n - 1):pb + tl, :]


def conv_mixer_mid(proj, hist_a, hist_b, conv_a_w, conv_b_w, conv_b_bias, ln_g, ln_b):
    nb, L, pw = proj.shape
    wa_n, da = conv_a_w.shape
    wb_n, db = conv_b_w.shape
    assert wa_n - 1 <= CONV_PAD_A and wb_n - 1 <= CONV_PAD_B
    tl = _tile(L, 256)
    assert tl == L or tl >= wb_n - 1
    kern = functools.partial(_conv_kernel, tl=tl, da=da, db=db, wa_n=wa_n, wb_n=wb_n)
    full2 = lambda b, i: (0, 0)
    return pl.pallas_call(
        kern,
        grid=(nb, L // tl),
        in_specs=[pl.BlockSpec((1, tl, pw), lambda b, i: (b, i, 0)),
                  pl.BlockSpec((1, wa_n - 1, da), lambda b, i: (b, 0, 0)),
                  pl.BlockSpec((1, wb_n - 1, db), lambda b, i: (b, 0, 0)),
                  pl.BlockSpec((wa_n, da), full2),
                  pl.BlockSpec((wb_n, db), full2),
                  pl.BlockSpec((1, db), full2),
                  pl.BlockSpec((1, db), full2),
                  pl.BlockSpec((1, db), full2)],
        out_specs=[pl.BlockSpec((1, tl, da + db), lambda b, i: (b, i, 0)),
                   pl.BlockSpec((1, wa_n - 1, da), lambda b, i: (b, 0, 0)),
                   pl.BlockSpec((1, wb_n - 1, db), lambda b, i: (b, 0, 0))],
        out_shape=[jax.ShapeDtypeStruct((nb, L, da + db), BF16),
                   jax.ShapeDtypeStruct((nb, wa_n - 1, da), F32),
                   jax.ShapeDtypeStruct((nb, wb_n - 1, db), F32)],
        scratch_shapes=[pltpu.VMEM((CONV_PAD_A + tl, da), F32),
                        pltpu.VMEM((CONV_PAD_B + tl, db), F32),
                        pltpu.VMEM((tl, db), F32)],
        compiler_params=_cparams("arbitrary", "arbitrary"),
        name="conv_mixer_mid",
    )(proj, hist_a, hist_b, conv_a_w, conv_b_w, conv_b_bias.reshape(1, db),
      ln_g.reshape(1, db), ln_b.reshape(1, db))


def _fox_post_kernel(proj_ref, f_ref, bf_ref, qg_ref, kg_ref,
                     qn_ref, kn_ref, kb_ref, v_ref, vb_ref, lf_ref, *, d, nh, dh):
    qg = qg_ref[...]
    kg = kg_ref[...]
    for h in range(nh):
        cs = slice(h * dh, (h + 1) * dh)
        q = proj_ref[0, :, h * dh:(h + 1) * dh]
        qn = q * lax.rsqrt(jnp.mean(q * q, axis=-1, keepdims=True) + NORM_EPS) * qg
        qn_ref[0, :, cs] = qn.astype(qn_ref.dtype)
        k = proj_ref[0, :, d + h * dh:d + (h + 1) * dh]
        kn = k * lax.rsqrt(jnp.mean(k * k, axis=-1, keepdims=True) + NORM_EPS) * kg
        kn_ref[0, :, cs] = kn
        kb_ref[0, :, cs] = kn.astype(kb_ref.dtype)
    v = proj_ref[0, :, 2 * d:3 * d]
    v_ref[0] = v
    vb_ref[0] = v.astype(vb_ref.dtype)
    z = f_ref[0] + bf_ref[...]
    ls = jnp.minimum(z, 0.0) - jnp.log(1.0 + jnp.exp(-jnp.abs(z)))
    lf_ref[0] = ls[:, 0:nh]


def fox_post(proj, f, b_f_pad, q_g, k_g, nh):
    nb, L, d3 = proj.shape
    d = d3 // 3
    dh = d // nh
    tl = _tile(L, 256)
    kern = functools.partial(_fox_post_kernel, d=d, nh=nh, dh=dh)
    row = lambda w: pl.BlockSpec((1, tl, w), lambda b, i: (b, i, 0))
    vec = lambda w: pl.BlockSpec((1, w), lambda b, i: (0, 0))
    return pl.pallas_call(
        kern,
        grid=(nb, L // tl),
        in_specs=[row(d3), row(LANES), vec(LANES), vec(dh), vec(dh)],
        out_specs=[row(d), row(d), row(d), row(d), row(d), row(nh)],
        out_shape=[jax.ShapeDtypeStruct((nb, L, d), BF16),
                   jax.ShapeDtypeStruct((nb, L, d), F32),
                   jax.ShapeDtypeStruct((nb, L, d), BF16),
                   jax.ShapeDtypeStruct((nb, L, d), F32),
                   jax.ShapeDtypeStruct((nb, L, d), BF16),
                   jax.ShapeDtypeStruct((nb, L, nh), F32)],
        compiler_params=_cparams("arbitrary", "arbitrary"),
        name="fox_post",
    )(proj, f, b_f_pad, q_g.reshape(1, dh), k_g.reshape(1, dh))


def _attn_prompt_kernel(q_ref, k_ref, v_ref, lf_ref, o_ref, f_scr, *, L, tq, dh, scale):
    tri = (_iota((LANES, LANES), 0) <= _iota((LANES, LANES), 1)).astype(BF16)
    carry = jnp.zeros((1, 1), F32)
    for blk in range(L // LANES):
        x = lf_ref[0, :, blk * LANES:(blk + 1) * LANES]
        xb = jnp.broadcast_to(x, (SUBLANES, LANES))
        cs = _dot_f32_lhs(xb, tri)[0:1, :] + carry
        f_scr[:, blk * LANES:(blk + 1) * LANES] = cs
        carry = cs[:, LANES - 1:LANES]

    eye = _iota((tq, tq), 0) == _iota((tq, tq), 1)
    causal = _iota((tq, tq), 1) <= _iota((tq, tq), 0)
    for qi in range(L // tq):
        q0 = qi * tq
        nk = q0 + tq
        q = q_ref[0, q0:nk, :]
        fq_row = f_scr[:, q0:nk]
        fq_col = jnp.sum(jnp.where(eye, jnp.broadcast_to(fq_row, (tq, tq)), 0.0),
                         axis=1, keepdims=True)
        s = _dot_nt(q, k_ref[0, 0:nk, :]) * scale + (fq_col - f_scr[:, 0:nk])
        s_diag = jnp.where(causal, s[:, q0:nk], NEG)
        m = jnp.max(s_diag, axis=1, keepdims=True)
        if qi:
            m = jnp.maximum(m, jnp.max(s[:, 0:q0], axis=1, keepdims=True))
        p_diag = jnp.exp(s_diag - m)
        l = jnp.sum(p_diag, axis=1, keepdims=True)
        acc = _dot(p_diag.astype(BF16), v_ref[0, q0:nk, :])
        if qi:
            p_left = jnp.exp(s[:, 0:q0] - m)
            l = l + jnp.sum(p_left, axis=1, keepdims=True)
            acc = acc + _dot(p_left.astype(BF16), v_ref[0, 0:q0, :])
        o_ref[0, q0:nk, :] = (acc / l).astype(o_ref.dtype)


def attn_prompt(qn, kb, vb, lf_t, nh):
    nb, L, d = qn.shape
    dh = d // nh
    tq = _tile(L, 256)
    assert tq % LANES == 0
    kern = functools.partial(_attn_prompt_kernel, L=L, tq=tq, dh=dh, scale=dh ** -0.5)
    hd = pl.BlockSpec((1, L, dh), lambda b, h: (b, 0, h))
    return pl.pallas_call(
        kern,
        grid=(nb, nh),
        in_specs=[hd, hd, hd, pl.BlockSpec((1, 1, L), lambda b, h: (b * nh + h, 0, 0))],
        out_specs=hd,
        out_shape=jax.ShapeDtypeStruct((nb, L, d), BF16),
        scratch_shapes=[pltpu.VMEM((1, L), F32)],
        compiler_params=_cparams("arbitrary", "arbitrary"),
        name="attn_prompt",
    )(qn, kb, vb, lf_t)


PAGES_PER_STEP = 2


def _attn_sample_kernel(pt_ref, q_ref, *refs, t_new, nh, dh, page, scale, ppb):
    del pt_ref
    cache_refs = refs[:3 * ppb]
    kn_ref, vn_ref, lfn_ref, o_ref, m_scr, l_scr, acc, car, neg_scr = refs[3 * ppb:]
    p = pl.program_id(1)
    last = pl.num_programs(1) - 1
    rows = t_new * nh
    flat = page * nh

    @pl.when(p == 0)
    def _():
        m_scr[...] = jnp.full(m_scr.shape, NEG, F32)
        l_scr[...] = jnp.zeros(l_scr.shape, F32)
        acc[...] = jnp.zeros(acc.shape, F32)
        car[...] = jnp.zeros(car.shape, F32)
        same_head = _iota((rows, flat), 0) % nh == _iota((rows, flat), 1) % nh
        neg_scr[...] = jnp.where(same_head, 0.0, NEG)

    q = q_ref[0]

    def online(s, v):
        m = m_scr[...]
        m_new = jnp.maximum(m, jnp.max(s, axis=1, keepdims=True))
        alpha = jnp.exp(m - m_new)
        pr = jnp.exp(s - m_new)
        l_scr[...] = alpha * l_scr[...] + jnp.sum(pr, axis=1, keepdims=True)
        acc[...] = alpha * acc[...] + _dot(pr.astype(BF16), v)
        m_scr[...] = m_new

    lane = _iota((1, flat), 1)
    for j in range(ppb):
        kc_ref, vc_ref, lf_ref = cache_refs[3 * j:3 * j + 3]
        lp = lf_ref[0, 0]
        suf = lp
        tot = lp
        step = nh
        while step < flat:
            suf = suf + jnp.where(lane < flat - step, pltpu.roll(suf, flat - step, 1), 0.0)
            tot = tot + pltpu.roll(tot, step, 1)
            step *= 2
        bias = (suf - lp) + car[...]
        car[...] = car[...] + tot
        kp = kc_ref[0, 0].reshape(flat, dh).astype(BF16)
        vp = vc_ref[0, 0].reshape(flat, dh).astype(BF16)
        online(_dot_nt(q, kp) * scale + bias + neg_scr[...], vp)

    @pl.when(p == last)
    def _():
        lane_n = _iota((1, LANES), 1)
        g = lfn_ref[0]
        step = nh
        while step < rows:
            g = g + jnp.where(lane_n >= step, pltpu.roll(g, step, 1), 0.0)
            step *= 2
        s = _dot_nt(q, kn_ref[0]) * scale - g
        rr = _iota((rows, LANES), 0)
        cc = _iota((rows, LANES), 1)
        ok = (cc % nh == rr % nh) & (cc // nh <= rr // nh) & (cc < rows)
        online(jnp.where(ok, s, NEG), vn_ref[0])
        o_ref[0] = (acc[...] / l_scr[...]).astype(o_ref.dtype)


def attn_sample(layer, qn, kb, vb, lf_new, cache_k, cache_v, cache_lf_flat, page_table, nh):
    nb, t_new, hd = qn.shape
    dh = hd // nh
    rows = t_new * nh
    n_pages = page_table.shape[1]
    page = cache_k.shape[2]
    flat = page * nh
    assert rows <= LANES
    ppb = PAGES_PER_STEP if n_pages % PAGES_PER_STEP == 0 else 1
    to_rows = lambda a: a.reshape(nb, rows, dh)
    pad_rows = lambda a: jnp.pad(to_rows(a), ((0, 0), (0, LANES - rows), (0, 0)))
    lfn = jnp.pad(lf_new.reshape(nb, 1, rows), ((0, 0), (0, 0), (0, LANES - rows)))
    kern = functools.partial(_attn_sample_kernel, t_new=t_new, nh=nh, dh=dh, page=page,
                             scale=dh ** -0.5, ppb=ppb)
    per_b = lambda r, w: pl.BlockSpec((1, r, w), lambda b, p, pt: (b, 0, 0))
    in_specs = [per_b(rows, dh)]
    args = [to_rows(qn)]
    for j in range(ppb):
        kv_map = lambda b, p, pt, j=j: (layer, pt[b, n_pages - 1 - (p * ppb + j)], 0, 0, 0)
        lf_map = lambda b, p, pt, j=j: (layer, pt[b, n_pages - 1 - (p * ppb + j)], 0, 0)
        in_specs += [pl.BlockSpec((1, 1, page, nh, dh), kv_map),
                     pl.BlockSpec((1, 1, page, nh, dh), kv_map),
                     pl.BlockSpec((1, 1, 1, flat), lf_map)]
        args += [cache_k, cache_v, cache_lf_flat]
    in_specs += [per_b(LANES, dh), per_b(LANES, dh), per_b(1, LANES)]
    args += [pad_rows(kb), pad_rows(vb), lfn]
    grid_spec = pltpu.PrefetchScalarGridSpec(
        num_scalar_prefetch=1,
        grid=(nb, n_pages // ppb),
        in_specs=in_specs,
        out_specs=per_b(rows, dh),
        scratch_shapes=[pltpu.VMEM((rows, 1), F32),
                        pltpu.VMEM((rows, 1), F32),
                        pltpu.VMEM((rows, dh), F32),
                        pltpu.VMEM((1, flat), F32),
                        pltpu.VMEM((rows, flat), F32)])
    out = pl.pallas_call(
        kern,
        grid_spec=grid_spec,
        out_shape=jax.ShapeDtypeStruct((nb, rows, dh), BF16),
        compiler_params=_cparams("arbitrary", "arbitrary"),
        name="attn_sample",
    )(page_table, *args)
    return out.reshape(nb, t_new, hd)


def _staircase(k):
    return [(r0, r1) for r0 in range(k) for r1 in range(k) if (r0 + 1) * (r1 + 1) <= k]


BF16_ROWS = 2 * SUBLANES


def _bf16_pair_words(x):
    bits = pltpu.bitcast(x.astype(BF16).astype(F32), jnp.uint32)
    return bits | (bits >> 16)


def _bf16_row_broadcast(words_row):
    return pltpu.bitcast(jnp.broadcast_to(words_row, (SUBLANES, words_row.shape[1])), BF16)


def _peer_select_kernel(qt_ref, keys_ref, a_ref, cnt_ref, rank1_ref, bz_ref, cand_scr, *, dk, topk):
    pairs = _staircase(topk)
    cand_scr[...] = jnp.full(cand_scr.shape, -jnp.inf, F32)

    def head(h, _):
        tops, ranks, scores = [], [], []
        for half in (0, 1):
            r0 = pl.multiple_of((h * 2 + half) * dk, dk)
            s = _dot(keys_ref[h, half], qt_ref[pl.ds(r0, dk), :])
            vals = []
            x = s
            rank = jnp.full(s.shape, float(topk), F32)
            for r in range(topk):
                mx = jnp.max(x, axis=0, keepdims=True)
                hit = x == mx
                rank = jnp.where(hit, float(r), rank)
                x = jnp.where(hit, -jnp.inf, x)
                vals.append(mx)
            tops.append(vals)
            ranks.append(rank)
            scores.append(s)
        a, b = tops
        for idx, (r0, r1) in enumerate(pairs):
            cand_scr[idx:idx + 1, :] = a[r0] + b[r1]
        cand = cand_scr[...]
        x = cand
        for _r in range(topk - 1):
            mx = jnp.max(x, axis=0, keepdims=True)
            x = jnp.where(x == mx, -jnp.inf, x)
        theta = jnp.max(x, axis=0, keepdims=True)
        sel = cand >= theta
        z = jnp.sum(jnp.where(sel, jnp.exp(cand - (a[0] + b[0])), 0.0), axis=0, keepdims=True)
        sel_f = jnp.where(sel, 1.0, 0.0)
        cnt = jnp.zeros(scores[0].shape, F32)
        idx = 0
        for r0 in range(topk):
            n = topk // (r0 + 1)
            cnt_r = jnp.sum(sel_f[idx:idx + n, :], axis=0, keepdims=True)
            cnt = jnp.where(ranks[0] == float(r0), cnt_r, cnt)
            idx += n
        nk, ts = scores[0].shape
        a_ref[h] = _bf16_pair_words(jnp.exp(scores[0] - a[0]))
        cnt_ref[h] = _bf16_pair_words(cnt)
        rank1_ref[h] = ranks[1].astype(BF16).reshape(nk // BF16_ROWS, BF16_ROWS, ts)
        bz_ref[h] = (jnp.exp(scores[1] - b[0]) / z).astype(BF16).reshape(
            nk // BF16_ROWS, BF16_ROWS, ts)
        return 0

    lax.fori_loop(0, keys_ref.shape[0], head, 0)


def peer_select(qt, keys):
    nh, _, nk, dk = keys.shape
    tp = qt.shape[1]
    ts = _tile(tp, 256)
    kern = functools.partial(_peer_select_kernel, dk=dk, topk=PEER_TOPK)
    n_cand = -(-len(_staircase(PEER_TOPK)) // SUBLANES) * SUBLANES
    f_spec = pl.BlockSpec((nh, nk, ts), lambda i: (0, 0, i))
    b_spec = pl.BlockSpec((nh, nk // BF16_ROWS, BF16_ROWS, ts), lambda i: (0, 0, 0, i))
    b_shape = jax.ShapeDtypeStruct((nh, nk // BF16_ROWS, BF16_ROWS, tp), BF16)
    return pl.pallas_call(
        kern,
        grid=(tp // ts,),
        in_specs=[pl.BlockSpec((nh * 2 * dk, ts), lambda i: (0, i)),
                  pl.BlockSpec((nh, 2, nk, dk), lambda i: (0, 0, 0, 0))],
        out_specs=[f_spec, f_spec, b_spec, b_spec],
        out_shape=[jax.ShapeDtypeStruct((nh, nk, tp), jnp.uint32),
                   jax.ShapeDtypeStruct((nh, nk, tp), jnp.uint32),
                   b_shape, b_shape],
        scratch_shapes=[pltpu.VMEM((n_cand, ts), F32)],
        compiler_params=_cparams("arbitrary"),
        name="peer_select",
    )(qt, keys)


GELU_K0 = -2.0 * GELU_C
GELU_K1 = -2.0 * GELU_C * 0.044715
PEER_SUB_BLOCKS = 2


def _peer_dense_kernel(xt_ref, u_ref, vt_ref, a_ref, cnt_ref, rank1_ref, bz_ref, o_ref,
                       wact, acc, *, nh, nk, nib, sub):
    e = pl.program_id(1)
    last = pl.num_programs(1) - 1
    tm = xt_ref.shape[1]
    groups = nk // BF16_ROWS

    @pl.when(e == 0)
    def _():
        acc[...] = jnp.zeros(acc.shape, F32)

    for sb in range(nib // sub):
        act = _dot(u_ref[sb * sub * nk:(sb + 1) * sub * nk, :], xt_ref[...])
        g = (act / (1.0 + jnp.exp(act * (GELU_K0 + GELU_K1 * (act * act))))).astype(BF16)
        for ii in range(sub):
            ib = sb * sub + ii
            for tc in range(tm // LANES):
                cs = slice(tc * LANES, (tc + 1) * LANES)
                w = jnp.zeros((groups, BF16_ROWS, LANES), BF16)
                for h in range(nh):
                    cnt = _bf16_row_broadcast(cnt_ref[h, ib:ib + 1, cs])
                    a = _bf16_row_broadcast(a_ref[h, ib:ib + 1, cs])
                    gate = a[None] * bz_ref[h, :, :, cs]
                    w = w + jnp.where(rank1_ref[h, :, :, cs] < cnt[None], gate, jnp.zeros_like(gate))
                wact[ib * nk:(ib + 1) * nk, cs] = w.reshape(nk, LANES) * g[ii * nk:(ii + 1) * nk, cs]
    acc[...] += _dot(vt_ref[...], wact[...])

    @pl.when(e == last)
    def _():
        o_ref[...] = acc[...]


def peer_dense(xt, u_b, vt_b, a, cnt, rank1, bz, tm_pref=512, ec_pref=1024):
    d, tp = xt.shape
    ne = u_b.shape[0]
    nh, nk, _ = a.shape
    tm = _tile(tp, tm_pref)
    ec = _tile(ne, ec_pref)
    nib = ec // nk
    assert nib % SUBLANES == 0 or nib == nk
    sub = PEER_SUB_BLOCKS
    assert nib % sub == 0 and tm % LANES == 0
    kern = functools.partial(_peer_dense_kernel, nh=nh, nk=nk, nib=nib, sub=sub)
    row_spec = pl.BlockSpec((nh, nib, tm), lambda i, e: (0, e, i))
    key_spec = pl.BlockSpec((nh, nk // BF16_ROWS, BF16_ROWS, tm), lambda i, e: (0, 0, 0, i))
    return pl.pallas_call(
        kern,
        grid=(tp // tm, ne // ec),
        in_specs=[pl.BlockSpec((d, tm), lambda i, e: (0, i)),
                  pl.BlockSpec((ec, d), lambda i, e: (e, 0)),
                  pl.BlockSpec((d, ec), lambda i, e: (0, e)),
                  row_spec, row_spec, key_spec, key_spec],
        out_specs=pl.BlockSpec((d, tm), lambda i, e: (0, i)),
        out_shape=jax.ShapeDtypeStruct((d, tp), F32),
        scratch_shapes=[pltpu.VMEM((ec, tm), BF16),
                        pltpu.VMEM((d, tm), F32)],
        compiler_params=_cparams("arbitrary", "arbitrary"),
        name="peer_dense",
    )(xt, u_b, vt_b, a, cnt, rank1, bz)


def peer_mixer_t(ht, wq_t, keys_b, u_b, vt_b):
    qt = matmul(wq_t[None], ht, out_dtype=BF16)[0]
    a, cnt, rank1, bz = peer_select(qt, keys_b)
    return peer_dense(ht, u_b, vt_b, a, cnt, rank1, bz)


def kernel(x_prompt, x_sample, state_conv_a, state_conv_b, cache_k, cache_v, cache_logf, page_table, c_prompt, c_sample, norm1_g, norm2_g, ada_w, ada_b, ab_w_in, ab_conv_a_w, ab_conv_b_w, ab_conv_b_bias, ab_ln_g, ab_ln_b, ab_w_out, attn_w_qkvf, attn_b_f, attn_q_norm_g, attn_k_norm_g, attn_w_o, peer_w_q, peer_sub_keys, peer_u, peer_v):
    n_p, seq, d = x_prompt.shape
    n_s, t_new, _ = x_sample.shape
    depth = ada_w.shape[0]
    nh = attn_b_f.shape[1]
    da = ab_conv_a_w.shape[2]
    db = ab_conv_b_w.shape[2]
    wa_n = ab_conv_a_w.shape[1]
    wb_n = ab_conv_b_w.shape[1]
    n_layers_attn, n_pool, page = cache_k.shape[:3]
    ts_tok = n_s * t_new
    tp_s = -(-ts_tok // LANES) * LANES

    n_c = n_p + n_s
    c_rows = -(-n_c // SUBLANES) * SUBLANES
    c_all = jnp.pad(jnp.concatenate([c_prompt, c_sample], axis=0), ((0, c_rows - n_c), (0, 0)))
    mod_all = ada_modulation(c_all, ada_w, ada_b)

    xp = x_prompt
    xs = x_sample.reshape(1, ts_tok, d)
    cache_lf_flat = cache_logf.reshape(n_layers_attn, n_pool, 1, page * nh)

    conv_a_p, conv_a_s, conv_b_p, conv_b_s = [], [], [], []
    k_p, v_p, lf_p, k_s, v_s, lf_s = [], [], [], [], [], []
    for l in range(depth):
        mods_p = [mod_all[l, :n_p, k * d:(k + 1) * d][:, None, :] for k in range(6)]
        mods_s = [jnp.repeat(mod_all[l, n_p:n_c, k * d:(k + 1) * d], t_new, axis=0)[None]
                  for k in range(6)]
        sh1p, sc1p, g1p, sh2p, sc2p, g2p = mods_p
        sh1s, sc1s, g1s, sh2s, sc2s, g2s = mods_s
        hp = modnorm(xp, norm1_g[l], sh1p, sc1p)
        hs = modnorm(xs, norm1_g[l], sh1s, sc1s)
        i = l // 2
        if l % 2 == 0:
            w_in = ab_w_in[i].astype(BF16)
            w_out = ab_w_out[i].astype(BF16)
            cw = (ab_conv_a_w[i], ab_conv_b_w[i], ab_conv_b_bias[i], ab_ln_g[i], ab_ln_b[i])
            proj_p = matmul(hp, w_in)
            yp, na, nb = conv_mixer_mid(proj_p, jnp.zeros((n_p, wa_n - 1, da), F32),
                                        jnp.zeros((n_p, wb_n - 1, db), F32), *cw)
            conv_a_p.append(na)
            conv_b_p.append(nb)
            xp = matmul(yp, w_out, res=xp, gate=g1p)
            proj_s = matmul(hs, w_in).reshape(n_s, t_new, -1)
            ys, na, nb = conv_mixer_mid(proj_s, state_conv_a[i], state_conv_b[i], *cw)
            conv_a_s.append(na)
            conv_b_s.append(nb)
            xs = matmul(ys.reshape(1, ts_tok, da + db), w_out, res=xs, gate=g1s)
        else:
            w_qkv = attn_w_qkvf[i][:, :3 * d].astype(BF16)
            w_f = jnp.pad(attn_w_qkvf[i][:, 3 * d:], ((0, 0), (0, LANES - nh))).astype(BF16)
            b_f = jnp.pad(attn_b_f[i], (0, LANES - nh)).reshape(1, LANES)
            w_o = attn_w_o[i].astype(BF16)
            qn, kn, kb, v, vb, lf = fox_post(matmul(hp, w_qkv), matmul(hp, w_f), b_f,
                                             attn_q_norm_g[i], attn_k_norm_g[i], nh)
            lf_t = jnp.transpose(lf, (0, 2, 1)).reshape(n_p * nh, 1, seq)
            op = attn_prompt(qn, kb, vb, lf_t, nh)
            xp = matmul(op, w_o, res=xp, gate=g1p)
            k_p.append(kn.reshape(n_p, seq, nh, d // nh))
            v_p.append(v.reshape(n_p, seq, nh, d // nh))
            lf_p.append(lf)
            qn, kn, kb, v, vb, lf = fox_post(matmul(hs, w_qkv), matmul(hs, w_f), b_f,
                                             attn_q_norm_g[i], attn_k_norm_g[i], nh)
            r3 = lambda a: a.reshape(n_s, t_new, a.shape[-1])
            os_ = attn_sample(i, r3(qn), r3(kb), r3(vb), r3(lf), cache_k, cache_v, cache_lf_flat,
                              page_table, nh)
            xs = matmul(os_.reshape(1, ts_tok, d), w_o, res=xs, gate=g1s)
            k_s.append(kn.reshape(n_s, t_new, nh, d // nh))
            v_s.append(v.reshape(n_s, t_new, nh, d // nh))
            lf_s.append(r3(lf))
        wq_t = jnp.transpose(peer_w_q[l]).astype(BF16)
        keys_b = peer_sub_keys[l].astype(BF16)
        u_b = peer_u[l].astype(BF16)
        vt_b = jnp.transpose(peer_v[l]).astype(BF16)
        h2p = modnorm(xp, norm2_g[l], sh2p, sc2p)
        yt = peer_mixer_t(jnp.transpose(h2p.reshape(n_p * seq, d)), wq_t, keys_b, u_b, vt_b)
        xp = gated_add(xp, g2p, jnp.transpose(yt).reshape(n_p, seq, d))
        h2s = modnorm(xs, norm2_g[l], sh2s, sc2s)
        h2s_t = jnp.pad(jnp.transpose(h2s[0]), ((0, 0), (0, tp_s - ts_tok)))
        yt = peer_mixer_t(h2s_t, wq_t, keys_b, u_b, vt_b)
        xs = gated_add(xs, g2s, jnp.transpose(yt[:, :ts_tok])[None])

    return (xp, xs.reshape(n_s, t_new, d), jnp.stack(conv_a_p), jnp.stack(conv_a_s),
            jnp.stack(conv_b_p), jnp.stack(conv_b_s),
            jnp.stack(k_p), jnp.stack(v_p), jnp.stack(lf_p),
            jnp.stack(k_s), jnp.stack(v_s), jnp.stack(lf_s))
```

```python
import functools
import math

import jax
import jax.numpy as jnp
from jax import lax
from jax.experimental import pallas as pl
from jax.experimental.pallas import tpu as pltpu

F32 = jnp.float32
BF16 = jnp.bfloat16
NORM_EPS = 1e-6
PEER_TOPK = 16
LANES = 128
SUBLANES = 8
VMEM_LIMIT = 56 * 1024 * 1024
NEG = -1e30
GELU_C = math.sqrt(2.0 / math.pi)


def _cparams(*sem):
    return pltpu.CompilerParams(dimension_semantics=sem, vmem_limit_bytes=VMEM_LIMIT)


def _tile(n, pref):
    if n <= pref:
        return n
    t = pref
    while n % t:
        t //= 2
    return t


def _sigmoid(x):
    return 1.0 / (1.0 + jnp.exp(-x))


def _dot(a, b):
    return jnp.dot(a, b, preferred_element_type=F32)


def _dot_nt(a, b):
    return lax.dot_general(a, b, (((1,), (1,)), ((), ())), preferred_element_type=F32)


def _split3(x):
    hi = x.astype(BF16)
    r1 = x - hi.astype(F32)
    mid = r1.astype(BF16)
    lo = (r1 - mid.astype(F32)).astype(BF16)
    return hi, mid, lo


def _dot_f32_lhs(x, m01):
    hi, mid, lo = _split3(x)
    return _dot(hi, m01) + _dot(mid, m01) + _dot(lo, m01)


def _dot_f32_rhs(m01, x):
    hi, mid, lo = _split3(x)
    return _dot(m01, hi) + _dot(m01, mid) + _dot(m01, lo)


def _iota(shape, dim):
    return lax.broadcasted_iota(jnp.int32, shape, dim)


def _ada_kernel(c_ref, w_ref, b_ref, o_ref):
    c = c_ref[...]
    sc = (c * _sigmoid(c)).astype(BF16)
    o_ref[0] = _dot(sc, w_ref[0].astype(BF16)) + b_ref[0]


def ada_modulation(c_all, ada_w, ada_b):
    depth, d, n = ada_w.shape
    rows = c_all.shape[0]
    tn = _tile(n, 1024)
    return pl.pallas_call(
        _ada_kernel,
        grid=(depth, n // tn),
        in_specs=[pl.BlockSpec((rows, d), lambda l, j: (0, 0)),
                  pl.BlockSpec((1, d, tn), lambda l, j: (l, 0, j)),
                  pl.BlockSpec((1, 1, tn), lambda l, j: (l, 0, j))],
        out_specs=pl.BlockSpec((1, rows, tn), lambda l, j: (l, 0, j)),
        out_shape=jax.ShapeDtypeStruct((depth, rows, n), F32),
        compiler_params=_cparams("arbitrary", "arbitrary"),
        name="ada_modulation",
    )(c_all, ada_w, ada_b.reshape(depth, 1, n))


def _modnorm_kernel(x_ref, g_ref, sh_ref, sc_ref, o_ref):
    x = x_ref[0]
    ms = jnp.mean(x * x, axis=-1, keepdims=True)
    y = x * lax.rsqrt(ms + NORM_EPS) * g_ref[...]
    o_ref[0] = (y * (1.0 + sc_ref[0]) + sh_ref[0]).astype(o_ref.dtype)


def _mod_spec(mod, tl, d):
    if mod.shape[1] == 1:
        return pl.BlockSpec((1, 1, d), lambda b, i, *_: (b, 0, 0))
    return pl.BlockSpec((1, tl, d), lambda b, i, *_: (b, i, 0))


def modnorm(x, g, shift, scale):
    nb, L, d = x.shape
    tl = _tile(L, 512)
    return pl.pallas_call(
        _modnorm_kernel,
        grid=(nb, L // tl),
        in_specs=[pl.BlockSpec((1, tl, d), lambda b, i: (b, i, 0)),
                  pl.BlockSpec((1, d), lambda b, i: (0, 0)),
                  _mod_spec(shift, tl, d), _mod_spec(scale, tl, d)],
        out_specs=pl.BlockSpec((1, tl, d), lambda b, i: (b, i, 0)),
        out_shape=jax.ShapeDtypeStruct((nb, L, d), BF16),
        compiler_params=_cparams("arbitrary", "arbitrary"),
        name="modnorm",
    )(x, g.reshape(1, d), shift, scale)


def _mm_kernel(a_ref, w_ref, o_ref):
    o_ref[0] = _dot(a_ref[0], w_ref[...]).astype(o_ref.dtype)


def _mm_res_kernel(a_ref, w_ref, r_ref, g_ref, o_ref):
    o_ref[0] = r_ref[0] + g_ref[0] * _dot(a_ref[0], w_ref[...])


def matmul(a, w, out_dtype=F32, res=None, gate=None, tl_pref=1024, tn_pref=512):
    nb, L, k = a.shape
    n = w.shape[1]
    tl = _tile(L, tl_pref)
    tn = _tile(n, tn_pref)
    in_specs = [pl.BlockSpec((1, tl, k), lambda b, i, j: (b, i, 0)),
                pl.BlockSpec((k, tn), lambda b, i, j: (0, j))]
    args = [a, w]
    body = _mm_kernel
    if res is not None:
        if gate.shape[1] == 1:
            gspec = pl.BlockSpec((1, 1, tn), lambda b, i, j: (b, 0, j))
        else:
            gspec = pl.BlockSpec((1, tl, tn), lambda b, i, j: (b, i, j))
        in_specs += [pl.BlockSpec((1, tl, tn), lambda b, i, j: (b, i, j)), gspec]
        args += [res, gate]
        body = _mm_res_kernel
    return pl.pallas_call(
        body,
        grid=(nb, L // tl, n // tn),
        in_specs=in_specs,
        out_specs=pl.BlockSpec((1, tl, tn), lambda b, i, j: (b, i, j)),
        out_shape=jax.ShapeDtypeStruct((nb, L, n), out_dtype),
        compiler_params=_cparams("arbitrary", "arbitrary", "arbitrary"),
        name="matmul",
    )(*args)


CONV_PAD_A = SUBLANES
CONV_PAD_B = 4 * SUBLANES


def _conv_kernel(proj_ref, ha_ref, hb_ref, wa_ref, wb_ref, bias_ref, lng_ref, lnb_ref,
                 y_ref, na_ref, nb_ref, fa, fb, cb, *, tl, da, db, wa_n, wb_n):
    i = pl.program_id(1)
    last = pl.num_programs(1) - 1
    pa, pb = CONV_PAD_A, CONV_PAD_B

    @pl.when(i == 0)
    def _():
        fa[pa - (wa_n - 1):pa, :] = ha_ref[0]
        fb[pb - (wb_n - 1):pb, :] = hb_ref[0]

    b_gate = proj_ref[0, :, 0:da]
    c_gate = proj_ref[0, :, da:2 * da]
    x_a = proj_ref[0, :, 2 * da:3 * da]
    v_b = proj_ref[0, :, 3 * da:3 * da + db]
    g_b = proj_ref[0, :, 3 * da + db:3 * da + 2 * db]
    fa[pa:pa + tl, :] = c_gate * x_a
    fb[pb:pb + tl, :] = v_b * _sigmoid(g_b)

    acc = jnp.zeros((tl, da), F32)
    for k in range(wa_n):
        r0 = pa - (wa_n - 1) + k
        acc = acc + wa_ref[k:k + 1, :] * fa[r0:r0 + tl, :]
    y_ref[0, :, 0:da] = (b_gate * acc).astype(y_ref.dtype)

    for c in range(db // LANES):
        cs = slice(c * LANES, (c + 1) * LANES)
        accb = jnp.zeros((tl, LANES), F32)
        for k in range(wb_n):
            r0 = pb - (wb_n - 1) + k
            accb = accb + wb_ref[k:k + 1, cs] * fb[r0:r0 + tl, cs]
        cb[:, cs] = accb + bias_ref[:, cs]

    z = cb[...]
    mu = jnp.mean(z, axis=-1, keepdims=True)
    zc = z - mu
    var = jnp.mean(zc * zc, axis=-1, keepdims=True)
    zn = zc * lax.rsqrt(var + NORM_EPS) * lng_ref[...] + lnb_ref[...]
    y_ref[0, :, da:da + db] = (zn * _sigmoid(zn)).astype(y_ref.dtype)

    @pl.when(i == last)
    def _():
        na_ref[0] = fa[pa + tl - (wa_n - 1):pa + tl, :]
        nb_ref[0] = fb[pb + tl - (wb_n - 1):pb + tl, :]

    @pl.when(i < last)
    def _():
        fa[pa - (wa_n - 1):pa, :] = fa[pa + tl - (wa_n - 1):pa + tl, :]
        fb[pb - (wb_n - 1):pb, :] = fb[pb + tl - (wb_n - 1):pb + tl, :]


def conv_mixer_mid(proj, hist_a, hist_b, conv_a_w, conv_b_w, conv_b_bias, ln_g, ln_b):
    nb, L, pw = proj.shape
    wa_n, da = conv_a_w.shape
    wb_n, db = conv_b_w.shape
    assert wa_n - 1 <= CONV_PAD_A and wb_n - 1 <= CONV_PAD_B
    tl = _tile(L, 256)
    assert tl == L or tl >= wb_n - 1
    kern = functools.partial(_conv_kernel, tl=tl, da=da, db=db, wa_n=wa_n, wb_n=wb_n)
    full2 = lambda b, i: (0, 0)
    return pl.pallas_call(
        kern,
        grid=(nb, L // tl),
        in_specs=[pl.BlockSpec((1, tl, pw), lambda b, i: (b, i, 0)),
                  pl.BlockSpec((1, wa_n - 1, da), lambda b, i: (b, 0, 0)),
                  pl.BlockSpec((1, wb_n - 1, db), lambda b, i: (b, 0, 0)),
                  pl.BlockSpec((wa_n, da), full2),
                  pl.BlockSpec((wb_n, db), full2),
                  pl.BlockSpec((1, db), full2),
                  pl.BlockSpec((1, db), full2),
                  pl.BlockSpec((1, db), full2)],
        out_specs=[pl.BlockSpec((1, tl, da + db), lambda b, i: (b, i, 0)),
                   pl.BlockSpec((1, wa_n - 1, da), lambda b, i: (b, 0, 0)),
                   pl.BlockSpec((1, wb_n - 1, db), lambda b, i: (b, 0, 0))],
        out_shape=[jax.ShapeDtypeStruct((nb, L, da + db), BF16),
                   jax.ShapeDtypeStruct((nb, wa_n - 1, da), F32),
                   jax.ShapeDtypeStruct((nb, wb_n - 1, db), F32)],
        scratch_shapes=[pltpu.VMEM((CONV_PAD_A + tl, da), F32),
                        pltpu.VMEM((CONV_PAD_B + tl, db), F32),
                        pltpu.VMEM((tl, db), F32)],
        compiler_params=_cparams("arbitrary", "arbitrary"),
        name="conv_mixer_mid",
    )(proj, hist_a, hist_b, conv_a_w, conv_b_w, conv_b_bias.reshape(1, db),
      ln_g.reshape(1, db), ln_b.reshape(1, db))


def _fox_post_kernel(proj_ref, f_ref, bf_ref, qg_ref, kg_ref,
                     qn_ref, kn_ref, kb_ref, v_ref, vb_ref, lf_ref, *, d, nh, dh):
    qg = qg_ref[...]
    kg = kg_ref[...]
    for h in range(nh):
        cs = slice(h * dh, (h + 1) * dh)
        q = proj_ref[0, :, h * dh:(h + 1) * dh]
        qn = q * lax.rsqrt(jnp.mean(q * q, axis=-1, keepdims=True) + NORM_EPS) * qg
        qn_ref[0, :, cs] = qn.astype(qn_ref.dtype)
        k = proj_ref[0, :, d + h * dh:d + (h + 1) * dh]
        kn = k * lax.rsqrt(jnp.mean(k * k, axis=-1, keepdims=True) + NORM_EPS) * kg
        kn_ref[0, :, cs] = kn
        kb_ref[0, :, cs] = kn.astype(kb_ref.dtype)
    v = proj_ref[0, :, 2 * d:3 * d]
    v_ref[0] = v
    vb_ref[0] = v.astype(vb_ref.dtype)
    z = f_ref[0] + bf_ref[...]
    ls = jnp.minimum(z, 0.0) - jnp.log(1.0 + jnp.exp(-jnp.abs(z)))
    lf_ref[0] = ls[:, 0:nh]


def fox_post(proj, f, b_f_pad, q_g, k_g, nh):
    nb, L, d3 = proj.shape
    d = d3 // 3
    dh = d // nh
    tl = _tile(L, 256)
    kern = functools.partial(_fox_post_kernel, d=d, nh=nh, dh=dh)
    row = lambda w: pl.BlockSpec((1, tl, w), lambda b, i: (b, i, 0))
    vec = lambda w: pl.BlockSpec((1, w), lambda b, i: (0, 0))
    return pl.pallas_call(
        kern,
        grid=(nb, L // tl),
        in_specs=[row(d3), row(LANES), vec(LANES), vec(dh), vec(dh)],
        out_specs=[row(d), row(d), row(d), row(d), row(d), row(nh)],
        out_shape=[jax.ShapeDtypeStruct((nb, L, d), BF16),
                   jax.ShapeDtypeStruct((nb, L, d), F32),
                   jax.ShapeDtypeStruct((nb, L, d), BF16),
                   jax.ShapeDtypeStruct((nb, L, d), F32),
                   jax.ShapeDtypeStruct((nb, L, d), BF16),
                   jax.ShapeDtypeStruct((nb, L, nh), F32)],
        compiler_params=_cparams("arbitrary", "arbitrary"),
        name="fox_post",
    )(proj, f, b_f_pad, q_g.reshape(1, dh), k_g.reshape(1, dh))


def _attn_prompt_kernel(q_ref, k_ref, v_ref, lf_ref, o_ref, f_scr, *, L, tq, dh, scale):
    tri = (_iota((LANES, LANES), 0) <= _iota((LANES, LANES), 1)).astype(BF16)
    carry = jnp.zeros((1, 1), F32)
    for blk in range(L // LANES):
        x = lf_ref[0, :, blk * LANES:(blk + 1) * LANES]
        xb = jnp.broadcast_to(x, (SUBLANES, LANES))
        cs = _dot_f32_lhs(xb, tri)[0:1, :] + carry
        f_scr[:, blk * LANES:(blk + 1) * LANES] = cs
        carry = cs[:, LANES - 1:LANES]

    eye = _iota((tq, tq), 0) == _iota((tq, tq), 1)
    causal = _iota((tq, tq), 1) <= _iota((tq, tq), 0)
    for qi in range(L // tq):
        q0 = qi * tq
        nk = q0 + tq
        q = q_ref[0, q0:nk, :]
        fq_row = f_scr[:, q0:nk]
        fq_col = jnp.sum(jnp.where(eye, jnp.broadcast_to(fq_row, (tq, tq)), 0.0),
                         axis=1, keepdims=True)
        s = _dot_nt(q, k_ref[0, 0:nk, :]) * scale + (fq_col - f_scr[:, 0:nk])
        s_diag = jnp.where(causal, s[:, q0:nk], NEG)
        m = jnp.max(s_diag, axis=1, keepdims=True)
        if qi:
            m = jnp.maximum(m, jnp.max(s[:, 0:q0], axis=1, keepdims=True))
        p_diag = jnp.exp(s_diag - m)
        l = jnp.sum(p_diag, axis=1, keepdims=True)
        acc = _dot(p_diag.astype(BF16), v_ref[0, q0:nk, :])
        if qi:
            p_left = jnp.exp(s[:, 0:q0] - m)
            l = l + jnp.sum(p_left, axis=1, keepdims=True)
            acc = acc + _dot(p_left.astype(BF16), v_ref[0, 0:q0, :])
        o_ref[0, q0:nk, :] = (acc / l).astype(o_ref.dtype)


def attn_prompt(qn, kb, vb, lf_t, nh):
    nb, L, d = qn.shape
    dh = d // nh
    tq = _tile(L, 256)
    assert tq % LANES == 0
    kern = functools.partial(_attn_prompt_kernel, L=L, tq=tq, dh=dh, scale=dh ** -0.5)
    hd = pl.BlockSpec((1, L, dh), lambda b, h: (b, 0, h))
    return pl.pallas_call(
        kern,
        grid=(nb, nh),
        in_specs=[hd, hd, hd, pl.BlockSpec((1, 1, L), lambda b, h: (b * nh + h, 0, 0))],
        out_specs=hd,
        out_shape=jax.ShapeDtypeStruct((nb, L, d), BF16),
        scratch_shapes=[pltpu.VMEM((1, L), F32)],
        compiler_params=_cparams("arbitrary", "arbitrary"),
        name="attn_prompt",
    )(qn, kb, vb, lf_t)


PAGES_PER_STEP = 4


def _attn_sample_kernel(pt_ref, q_ref, *refs, t_new, nh, dh, page, scale, ppb):
    del pt_ref
    cache_refs = refs[:3 * ppb]
    kn_ref, vn_ref, lfn_ref, o_ref, m_scr, l_scr, acc, car, neg_scr = refs[3 * ppb:]
    p = pl.program_id(1)
    last = pl.num_programs(1) - 1
    rows = t_new * nh
    flat = page * nh

    @pl.when(p == 0)
    def _():
        m_scr[...] = jnp.full(m_scr.shape, NEG, F32)
        l_scr[...] = jnp.zeros(l_scr.shape, F32)
        acc[...] = jnp.zeros(acc.shape, F32)
        car[...] = jnp.zeros(car.shape, F32)
        same_head = _iota((rows, flat), 0) % nh == _iota((rows, flat), 1) % nh
        neg_scr[...] = jnp.where(same_head, 0.0, NEG)

    q = q_ref[0]

    def online(blocks):
        m = m_scr[...]
        m_new = m
        for s, _ in blocks:
            m_new = jnp.maximum(m_new, jnp.max(s, axis=1, keepdims=True))
        alpha = jnp.exp(m - m_new)
        l_new = alpha * l_scr[...]
        acc_new = alpha * acc[...]
        for s, v in blocks:
            pr = jnp.exp(s - m_new)
            l_new = l_new + jnp.sum(pr, axis=1, keepdims=True)
            acc_new = acc_new + _dot(pr.astype(BF16), v)
        l_scr[...] = l_new
        acc[...] = acc_new
        m_scr[...] = m_new

    lane = _iota((1, flat), 1)
    blocks = []
    for j in range(ppb):
        kc_ref, vc_ref, lf_ref = cache_refs[3 * j:3 * j + 3]
        lp = lf_ref[0, 0]
        suf = lp
        tot = lp
        step = nh
        while step < flat:
            suf = suf + jnp.where(lane < flat - step, pltpu.roll(suf, flat - step, 1), 0.0)
            tot = tot + pltpu.roll(tot, step, 1)
            step *= 2
        bias = (suf - lp) + car[...]
        car[...] = car[...] + tot
        kp = kc_ref[0, 0].reshape(flat, dh).astype(BF16)
        vp = vc_ref[0, 0].reshape(flat, dh).astype(BF16)
        blocks.append((_dot_nt(q, kp) * scale + bias + neg_scr[...], vp))
    online(blocks)

    @pl.when(p == last)
    def _():
        lane_n = _iota((1, LANES), 1)
        g = lfn_ref[0]
        step = nh
        while step < rows:
            g = g + jnp.where(lane_n >= step, pltpu.roll(g, step, 1), 0.0)
            step *= 2
        s = _dot_nt(q, kn_ref[0]) * scale - g
        rr = _iota((rows, LANES), 0)
        cc = _iota((rows, LANES), 1)
        ok = (cc % nh == rr % nh) & (cc // nh <= rr // nh) & (cc < rows)
        online([(jnp.where(ok, s, NEG), vn_ref[0])])
        o_ref[0] = (acc[...] / l_scr[...]).astype(o_ref.dtype)


def attn_sample(layer, qn, kb, vb, lf_new, cache_k, cache_v, cache_lf_flat, page_table, nh):
    nb, t_new, hd = qn.shape
    dh = hd // nh
    rows = t_new * nh
    n_pages = page_table.shape[1]
    page = cache_k.shape[2]
    flat = page * nh
    assert rows <= LANES
    ppb = PAGES_PER_STEP if n_pages % PAGES_PER_STEP == 0 else 1
    to_rows = lambda a: a.reshape(nb, rows, dh)
    pad_rows = lambda a: jnp.pad(to_rows(a), ((0, 0), (0, LANES - rows), (0, 0)))
    lfn = jnp.pad(lf_new.reshape(nb, 1, rows), ((0, 0), (0, 0), (0, LANES - rows)))
    kern = functools.partial(_attn_sample_kernel, t_new=t_new, nh=nh, dh=dh, page=page,
                             scale=dh ** -0.5, ppb=ppb)
    per_b = lambda r, w: pl.BlockSpec((1, r, w), lambda b, p, pt: (b, 0, 0))
    in_specs = [per_b(rows, dh)]
    args = [to_rows(qn)]
    for j in range(ppb):
        kv_map = lambda b, p, pt, j=j: (layer, pt[b, n_pages - 1 - (p * ppb + j)], 0, 0, 0)
        lf_map = lambda b, p, pt, j=j: (layer, pt[b, n_pages - 1 - (p * ppb + j)], 0, 0)
        in_specs += [pl.BlockSpec((1, 1, page, nh, dh), kv_map),
                     pl.BlockSpec((1, 1, page, nh, dh), kv_map),
                     pl.BlockSpec((1, 1, 1, flat), lf_map)]
        args += [cache_k, cache_v, cache_lf_flat]
    in_specs += [per_b(LANES, dh), per_b(LANES, dh), per_b(1, LANES)]
    args += [pad_rows(kb), pad_rows(vb), lfn]
    grid_spec = pltpu.PrefetchScalarGridSpec(
        num_scalar_prefetch=1,
        grid=(nb, n_pages // ppb),
        in_specs=in_specs,
        out_specs=per_b(rows, dh),
        scratch_shapes=[pltpu.VMEM((rows, 1), F32),
                        pltpu.VMEM((rows, 1), F32),
                        pltpu.VMEM((rows, dh), F32),
                        pltpu.VMEM((1, flat), F32),
                        pltpu.VMEM((rows, flat), F32)])
    out = pl.pallas_call(
        kern,
        grid_spec=grid_spec,
        out_shape=jax.ShapeDtypeStruct((nb, rows, dh), BF16),
        compiler_params=_cparams("arbitrary", "arbitrary"),
        name="attn_sample",
    )(page_table, *args)
    return out.reshape(nb, t_new, hd)


def _staircase(k):
    return [(r0, r1) for r0 in range(k) for r1 in range(k) if (r0 + 1) * (r1 + 1) <= k]


BF16_ROWS = 2 * SUBLANES


def _bf16_pair_words(x):
    bits = pltpu.bitcast(x.astype(BF16).astype(F32), jnp.uint32)
    return bits | (bits >> 16)


def _bf16_row_broadcast(words_row):
    return pltpu.bitcast(jnp.broadcast_to(words_row, (SUBLANES, words_row.shape[1])), BF16)


def _bf16_pack_halves(x):
    half = x.shape[0] // 2
    bits = pltpu.bitcast(x.astype(BF16).astype(F32), jnp.uint32)
    return (bits[:half] >> 16) | (bits[half:] & jnp.uint32(0xFFFF0000))


def _permute_packed_rows(w, nk):
    n, d = w.shape
    return w.reshape(n // nk, 2, nk // 2, d).transpose(0, 2, 1, 3).reshape(n, d)


def _peer_select_kernel(qt_ref, keys_ref, a_ref, cnt_ref, rank1_ref, bz_ref, cand_scr, *, dk, topk):
    pairs = _staircase(topk)
    cand_scr[...] = jnp.full(cand_scr.shape, -jnp.inf, F32)

    def head(h, _):
        tops, ranks, scores = [], [], []
        for half in (0, 1):
            r0 = pl.multiple_of((h * 2 + half) * dk, dk)
            s = _dot(keys_ref[h, half], qt_ref[pl.ds(r0, dk), :])
            vals = []
            x = s
            rank = jnp.full(s.shape, float(topk), F32)
            for r in range(topk):
                mx = jnp.max(x, axis=0, keepdims=True)
                hit = x == mx
                rank = jnp.where(hit, float(r), rank)
                x = jnp.where(hit, -jnp.inf, x)
                vals.append(mx)
            tops.append(vals)
            ranks.append(rank)
            scores.append(s)
        a, b = tops
        for idx, (r0, r1) in enumerate(pairs):
            cand_scr[idx:idx + 1, :] = a[r0] + b[r1]
        cand = cand_scr[...]
        x = cand
        for _r in range(topk - 1):
            mx = jnp.max(x, axis=0, keepdims=True)
            x = jnp.where(x == mx, -jnp.inf, x)
        theta = jnp.max(x, axis=0, keepdims=True)
        sel = cand >= theta
        z = jnp.sum(jnp.where(sel, jnp.exp(cand - (a[0] + b[0])), 0.0), axis=0, keepdims=True)
        sel_f = jnp.where(sel, 1.0, 0.0)
        cnt = jnp.zeros(scores[0].shape, F32)
        idx = 0
        for r0 in range(topk):
            n = topk // (r0 + 1)
            cnt_r = jnp.sum(sel_f[idx:idx + n, :], axis=0, keepdims=True)
            cnt = jnp.where(ranks[0] == float(r0), cnt_r, cnt)
            idx += n
        a_ref[h] = _bf16_pair_words(jnp.exp(scores[0] - a[0]))
        cnt_ref[h] = _bf16_pair_words(cnt)
        rank1_ref[h] = _bf16_pack_halves(ranks[1])
        bz_ref[h] = _bf16_pack_halves(jnp.exp(scores[1] - b[0]) / z)
        return 0

    lax.fori_loop(0, keys_ref.shape[0], head, 0)


def peer_select(qt, keys):
    nh, _, nk, dk = keys.shape
    tp = qt.shape[1]
    ts = _tile(tp, 256)
    kern = functools.partial(_peer_select_kernel, dk=dk, topk=PEER_TOPK)
    n_cand = -(-len(_staircase(PEER_TOPK)) // SUBLANES) * SUBLANES
    f_spec = pl.BlockSpec((nh, nk, ts), lambda i: (0, 0, i))
    b_spec = pl.BlockSpec((nh, nk // 2, ts), lambda i: (0, 0, i))
    b_shape = jax.ShapeDtypeStruct((nh, nk // 2, tp), jnp.uint32)
    return pl.pallas_call(
        kern,
        grid=(tp // ts,),
        in_specs=[pl.BlockSpec((nh * 2 * dk, ts), lambda i: (0, i)),
                  pl.BlockSpec((nh, 2, nk, dk), lambda i: (0, 0, 0, 0))],
        out_specs=[f_spec, f_spec, b_spec, b_spec],
        out_shape=[jax.ShapeDtypeStruct((nh, nk, tp), jnp.uint32),
                   jax.ShapeDtypeStruct((nh, nk, tp), jnp.uint32),
                   b_shape, b_shape],
        scratch_shapes=[pltpu.VMEM((n_cand, ts), F32)],
        compiler_params=_cparams("arbitrary"),
        name="peer_select",
    )(qt, keys)


GELU_K0 = -2.0 * GELU_C
GELU_K1 = -2.0 * GELU_C * 0.044715
PEER_SUB_BLOCKS = 2


def _peer_dense_kernel(xt_ref, u_ref, vt_ref, a_ref, cnt_ref, rank1_ref, bz_ref, res_ref, gate_ref,
                       o_ref, acc, *, nh, nk, nib, sub):
    e = pl.program_id(1)
    last = pl.num_programs(1) - 1
    tm = xt_ref.shape[1]
    groups = nk // BF16_ROWS

    @pl.when(e == 0)
    def _():
        acc[...] = jnp.zeros(acc.shape, F32)

    blocks = []
    for sb in range(nib // sub):
        act = _dot(u_ref[sb * sub * nk:(sb + 1) * sub * nk, :], xt_ref[...])
        g = (act / (1.0 + jnp.exp(act * (GELU_K0 + GELU_K1 * (act * act))))).astype(BF16)
        for ii in range(sub):
            ib = sb * sub + ii
            cols = []
            for tc in range(tm // LANES):
                cs = slice(tc * LANES, (tc + 1) * LANES)
                ws = [jnp.zeros((BF16_ROWS, LANES), BF16) for _ in range(groups)]
                for h in range(nh):
                    cnt = _bf16_row_broadcast(cnt_ref[h, ib:ib + 1, cs])
                    a = _bf16_row_broadcast(a_ref[h, ib:ib + 1, cs])
                    for gi in range(groups):
                        ws_rows = slice(gi * SUBLANES, (gi + 1) * SUBLANES)
                        gate = a * pltpu.bitcast(bz_ref[h, ws_rows, cs], BF16)
                        hit = pltpu.bitcast(rank1_ref[h, ws_rows, cs], BF16) < cnt
                        ws[gi] = ws[gi] + jnp.where(hit, gate, jnp.zeros_like(gate))
                cols.append(jnp.concatenate(
                    [ws[gi] * g[ii * nk + gi * BF16_ROWS:ii * nk + (gi + 1) * BF16_ROWS, cs]
                     for gi in range(groups)], axis=0))
            blocks.append(jnp.concatenate(cols, axis=1))
    acc[...] += _dot(vt_ref[...], jnp.concatenate(blocks, axis=0))

    @pl.when(e == last)
    def _():
        o_ref[...] = res_ref[...] + gate_ref[0] * acc[...].T


def peer_dense(xt, u_b, vt_b, a, cnt, rank1, bz, res, gate, tm_pref=512, ec_pref=1024):
    d, tp = xt.shape
    ne = u_b.shape[0]
    nh, nk, _ = a.shape
    tm = _tile(tp, tm_pref)
    ec = _tile(ne, ec_pref)
    nib = ec // nk
    assert nib % SUBLANES == 0 or nib == nk
    sub = PEER_SUB_BLOCKS
    assert nib % sub == 0 and tm % LANES == 0
    kern = functools.partial(_peer_dense_kernel, nh=nh, nk=nk, nib=nib, sub=sub)
    row_spec = pl.BlockSpec((nh, nib, tm), lambda i, e: (0, e, i))
    key_spec = pl.BlockSpec((nh, nk // 2, tm), lambda i, e: (0, 0, i))
    if gate.shape[1] == 1:
        rows_per_gate = tp // gate.shape[0]
        assert rows_per_gate % tm == 0
        gate_spec = pl.BlockSpec((1, 1, d), lambda i, e: (i * tm // rows_per_gate, 0, 0))
    else:
        gate_spec = pl.BlockSpec((1, tm, d), lambda i, e: (0, i, 0))
    out_spec = pl.BlockSpec((tm, d), lambda i, e: (i, 0))
    return pl.pallas_call(
        kern,
        grid=(tp // tm, ne // ec),
        in_specs=[pl.BlockSpec((d, tm), lambda i, e: (0, i)),
                  pl.BlockSpec((ec, d), lambda i, e: (e, 0)),
                  pl.BlockSpec((d, ec), lambda i, e: (0, e)),
                  row_spec, row_spec, key_spec, key_spec, out_spec, gate_spec],
        out_specs=out_spec,
        out_shape=jax.ShapeDtypeStruct((tp, d), F32),
        scratch_shapes=[pltpu.VMEM((d, tm), F32)],
        compiler_params=_cparams("arbitrary", "arbitrary"),
        name="peer_dense",
    )(xt, u_b, vt_b, a, cnt, rank1, bz, res, gate)


def peer_residual(ht, res, gate, wq_t, keys_b, u_b, vt_b):
    qt = matmul(wq_t[None], ht, out_dtype=BF16)[0]
    a, cnt, rank1, bz = peer_select(qt, keys_b)
    return peer_dense(ht, u_b, vt_b, a, cnt, rank1, bz, res, gate)


def kernel(x_prompt, x_sample, state_conv_a, state_conv_b, cache_k, cache_v, cache_logf, page_table, c_prompt, c_sample, norm1_g, norm2_g, ada_w, ada_b, ab_w_in, ab_conv_a_w, ab_conv_b_w, ab_conv_b_bias, ab_ln_g, ab_ln_b, ab_w_out, attn_w_qkvf, attn_b_f, attn_q_norm_g, attn_k_norm_g, attn_w_o, peer_w_q, peer_sub_keys, peer_u, peer_v):
    n_p, seq, d = x_prompt.shape
    n_s, t_new, _ = x_sample.shape
    depth = ada_w.shape[0]
    nh = attn_b_f.shape[1]
    da = ab_conv_a_w.shape[2]
    db = ab_conv_b_w.shape[2]
    wa_n = ab_conv_a_w.shape[1]
    wb_n = ab_conv_b_w.shape[1]
    n_layers_attn, n_pool, page = cache_k.shape[:3]
    ts_tok = n_s * t_new
    tp_s = -(-ts_tok // LANES) * LANES

    n_c = n_p + n_s
    c_rows = -(-n_c // SUBLANES) * SUBLANES
    c_all = jnp.pad(jnp.concatenate([c_prompt, c_sample], axis=0), ((0, c_rows - n_c), (0, 0)))
    mod_all = ada_modulation(c_all, ada_w, ada_b)

    xp = x_prompt
    xs = x_sample.reshape(1, ts_tok, d)
    cache_lf_flat = cache_logf.reshape(n_layers_attn, n_pool, 1, page * nh)

    conv_a_p, conv_a_s, conv_b_p, conv_b_s = [], [], [], []
    k_p, v_p, lf_p, k_s, v_s, lf_s = [], [], [], [], [], []
    for l in range(depth):
        mods_p = [mod_all[l, :n_p, k * d:(k + 1) * d][:, None, :] for k in range(6)]
        mods_s = [jnp.repeat(mod_all[l, n_p:n_c, k * d:(k + 1) * d], t_new, axis=0)[None]
                  for k in range(6)]
        sh1p, sc1p, g1p, sh2p, sc2p, g2p = mods_p
        sh1s, sc1s, g1s, sh2s, sc2s, g2s = mods_s
        hp = modnorm(xp, norm1_g[l], sh1p, sc1p)
        hs = modnorm(xs, norm1_g[l], sh1s, sc1s)
        i = l // 2
        if l % 2 == 0:
            w_in = ab_w_in[i].astype(BF16)
            w_out = ab_w_out[i].astype(BF16)
            cw = (ab_conv_a_w[i], ab_conv_b_w[i], ab_conv_b_bias[i], ab_ln_g[i], ab_ln_b[i])
            proj_p = matmul(hp, w_in)
            yp, na, nb = conv_mixer_mid(proj_p, jnp.zeros((n_p, wa_n - 1, da), F32),
                                        jnp.zeros((n_p, wb_n - 1, db), F32), *cw)
            conv_a_p.append(na)
            conv_b_p.append(nb)
            xp = matmul(yp, w_out, res=xp, gate=g1p)
            proj_s = matmul(hs, w_in).reshape(n_s, t_new, -1)
            ys, na, nb = conv_mixer_mid(proj_s, state_conv_a[i], state_conv_b[i], *cw)
            conv_a_s.append(na)
            conv_b_s.append(nb)
            xs = matmul(ys.reshape(1, ts_tok, da + db), w_out, res=xs, gate=g1s)
        else:
            w_qkv = attn_w_qkvf[i][:, :3 * d].astype(BF16)
            w_f = jnp.pad(attn_w_qkvf[i][:, 3 * d:], ((0, 0), (0, LANES - nh))).astype(BF16)
            b_f = jnp.pad(attn_b_f[i], (0, LANES - nh)).reshape(1, LANES)
            w_o = attn_w_o[i].astype(BF16)
            qn, kn, kb, v, vb, lf = fox_post(matmul(hp, w_qkv), matmul(hp, w_f), b_f,
                                             attn_q_norm_g[i], attn_k_norm_g[i], nh)
            lf_t = jnp.transpose(lf, (0, 2, 1)).reshape(n_p * nh, 1, seq)
            op = attn_prompt(qn, kb, vb, lf_t, nh)
            xp = matmul(op, w_o, res=xp, gate=g1p)
            k_p.append(kn.reshape(n_p, seq, nh, d // nh))
            v_p.append(v.reshape(n_p, seq, nh, d // nh))
            lf_p.append(lf)
            qn, kn, kb, v, vb, lf = fox_post(matmul(hs, w_qkv), matmul(hs, w_f), b_f,
                                             attn_q_norm_g[i], attn_k_norm_g[i], nh)
            r3 = lambda a: a.reshape(n_s, t_new, a.shape[-1])
            os_ = attn_sample(i, r3(qn), r3(kb), r3(vb), r3(lf), cache_k, cache_v, cache_lf_flat,
                              page_table, nh)
            xs = matmul(os_.reshape(1, ts_tok, d), w_o, res=xs, gate=g1s)
            k_s.append(kn.reshape(n_s, t_new, nh, d // nh))
            v_s.append(v.reshape(n_s, t_new, nh, d // nh))
            lf_s.append(r3(lf))
        wq_t = jnp.transpose(peer_w_q[l]).astype(BF16)
        keys_b = peer_sub_keys[l].astype(BF16)
        nk = peer_sub_keys.shape[3]
        u_b = _permute_packed_rows(peer_u[l], nk).astype(BF16)
        vt_b = jnp.transpose(_permute_packed_rows(peer_v[l], nk)).astype(BF16)
        h2p = modnorm(xp, norm2_g[l], sh2p, sc2p)
        xp = peer_residual(jnp.transpose(h2p.reshape(n_p * seq, d)), xp.reshape(n_p * seq, d), g2p,
                           wq_t, keys_b, u_b, vt_b).reshape(n_p, seq, d)
        h2s = modnorm(xs, norm2_g[l], sh2s, sc2s)
        h2s_t = jnp.pad(jnp.transpose(h2s[0]), ((0, 0), (0, tp_s - ts_tok)))
        pad_tok = lambda a_: jnp.pad(a_, ((0, 0), (0, tp_s - ts_tok), (0, 0)))
        xs = peer_residual(h2s_t, pad_tok(xs)[0], pad_tok(g2s),
                           wq_t, keys_b, u_b, vt_b)[None, :ts_tok]

    return (xp, xs.reshape(n_s, t_new, d), jnp.stack(conv_a_p), jnp.stack(conv_a_s),
            jnp.stack(conv_b_p), jnp.stack(conv_b_s),
            jnp.stack(k_p), jnp.stack(v_p), jnp.stack(lf_p),
            jnp.stack(k_s), jnp.stack(v_s), jnp.stack(lf_s))
```
